```python
import jax, jax.numpy as jnp
from jax import lax
import numpy as np

D_MODEL = 1024
BATCH = 2
SEQ = 8192
DEPTH = 2

GRID_W = 64
N_MEM = 256
EPS = 1e-6
POOL_WIDTH = D_MODEL // 2
POOL_GROUPS = 4
POOL_GROUP_W = POOL_WIDTH // POOL_GROUPS
POOL_WINDOWS = (2, 4, 8, 16)
MLA_HEADS = 8
MLA_NOPE = 64
MLA_ROPE = 32
MLA_V = 64
MLA_QK = MLA_NOPE + MLA_ROPE
Q_LORA = D_MODEL // 4
KV_LORA = D_MODEL // 8
ROPE_THETA = 10000.0
Q_BLOCK = 128
W_IN_EVEN = POOL_WIDTH + Q_LORA + KV_LORA + MLA_ROPE
MIX_EVEN = POOL_WIDTH + MLA_HEADS * MLA_V
NA_HEADS = 16
NA_HEAD_DIM = 64
NA_KH = 8
NA_KW = 16
NA_WIDTH = NA_HEADS * NA_HEAD_DIM
MEM_HEADS = 4
MEM_HEAD_DIM = D_MODEL // MEM_HEADS
D_FF = 4 * D_MODEL

kernel_name = "hybrid_pool_mla_natten_encoder"


def rms_norm(x, g):
    xf = x.astype(jnp.float32)
    y = xf * lax.rsqrt(jnp.mean(xf * xf, axis=-1, keepdims=True) + EPS)
    return (y * g.astype(jnp.float32)).astype(x.dtype)


def rotary(x, pos):
    half = x.shape[-1] // 2
    freqs = ROPE_THETA ** (-jnp.arange(half, dtype=jnp.float32) / half)
    ang = pos[:, None] * freqs[None, :]
    cos = jnp.cos(ang)[None, :, None, :].astype(x.dtype)
    sin = jnp.sin(ang)[None, :, None, :].astype(x.dtype)
    x1, x2 = x[..., :half], x[..., half:]
    return jnp.concatenate([x1 * cos - x2 * sin, x1 * sin + x2 * cos], axis=-1)


def pool_mixer(u, pool_w, pool_scale):
    B, S, _ = u.shape
    ug = u.reshape(B, S, POOL_GROUPS, POOL_GROUP_W).astype(jnp.float32)
    cs = jnp.concatenate([jnp.zeros((B, 1, POOL_GROUPS, POOL_GROUP_W), jnp.float32),
                          jnp.cumsum(ug, axis=1)], axis=1)
    t = jnp.arange(S)
    outs = []
    for g, w in enumerate(POOL_WINDOWS):
        lo = jnp.clip(t - w // 2, 0, S - 1)
        hi = jnp.clip(t + w - 1 - w // 2, 0, S - 1)
        sums = cs[:, hi + 1, g] - cs[:, lo, g]
        cnt = (hi - lo + 1).astype(jnp.float32)[None, :, None]
        outs.append(sums / cnt - ug[:, :, g])
    d = jnp.stack(outs, axis=2).astype(u.dtype)
    y = jnp.einsum('bsgc,gcd->bsgd', d, pool_w).reshape(B, S, POOL_WIDTH)
    return y * pool_scale


def blocked_attention(q, k, v):
    B, S, H, Dk = q.shape
    nb = S // Q_BLOCK
    scale = Dk ** -0.5
    qb = q.reshape(B, nb, Q_BLOCK, H, Dk).transpose(1, 0, 2, 3, 4)

    def one_block(qi):
        s = jnp.einsum('bqhd,bkhd->bhqk', qi, k).astype(jnp.float32) * scale
        p = jax.nn.softmax(s, axis=-1).astype(v.dtype)
        return jnp.einsum('bhqk,bkhd->bqhd', p, v)

    o = lax.map(one_block, qb)
    return o.transpose(1, 0, 2, 3, 4).reshape(B, S, H, v.shape[-1])


def mla_mixer(c_q, c_kv, k_r, q_lora_g, w_uq, kv_lora_g, w_ukv, q_g, k_g):
    B, S, _ = c_q.shape
    q = (rms_norm(c_q, q_lora_g) @ w_uq).reshape(B, S, MLA_HEADS, MLA_QK)
    kv = (rms_norm(c_kv, kv_lora_g) @ w_ukv).reshape(B, S, MLA_HEADS, MLA_NOPE + MLA_V)
    k_nope, v = kv[..., :MLA_NOPE], kv[..., MLA_NOPE:]
    k = jnp.concatenate([k_nope, jnp.broadcast_to(k_r[:, :, None, :], (B, S, MLA_HEADS, MLA_ROPE))], axis=-1)
    q = rms_norm(q, q_g)
    k = rms_norm(k, k_g)
    pos = jnp.arange(S, dtype=jnp.float32)
    q = jnp.concatenate([q[..., :MLA_NOPE], rotary(q[..., MLA_NOPE:], pos)], axis=-1)
    k = jnp.concatenate([k[..., :MLA_NOPE], rotary(k[..., MLA_NOPE:], pos)], axis=-1)
    o = blocked_attention(q, k, v)
    return o.reshape(B, S, MLA_HEADS * MLA_V)


def neighbourhood_attention(q, k, v, rpb):
    B, S, H, Dh = q.shape
    rows = S // GRID_W
    kh = min(NA_KH, rows)
    kw = NA_KW
    scale = Dh ** -0.5
    qg = q.reshape(B, rows, GRID_W, H, Dh).transpose(1, 0, 2, 3, 4)
    kg = k.reshape(B, rows, GRID_W, H, Dh)
    vg = v.reshape(B, rows, GRID_W, H, Dh)
    cols = jnp.arange(GRID_W)
    c0 = jnp.clip(cols - kw // 2, 0, GRID_W - kw)
    col_idx = c0[:, None] + jnp.arange(kw)[None, :]
    dc_idx = col_idx - cols[:, None] + (NA_KW - 1)

    def one_row(args):
        r, q_row = args
        r0 = jnp.clip(r - kh // 2, 0, rows - kh)
        k_rows = lax.dynamic_slice_in_dim(kg, r0, kh, axis=1)
        v_rows = lax.dynamic_slice_in_dim(vg, r0, kh, axis=1)
        k_win = k_rows[:, :, col_idx]
        v_win = v_rows[:, :, col_idx]
        s = jnp.einsum('bchd,brckhd->bhcrk', q_row, k_win).astype(jnp.float32) * scale
        dr_idx = r0 + jnp.arange(kh) - r + (NA_KH - 1)
        bias = rpb[:, dr_idx[:, None, None], dc_idx[None, :, :]]
        s = s + bias.transpose(0, 2, 1, 3)[None].astype(jnp.float32)
        p = jax.nn.softmax(s.reshape(B, H, GRID_W, kh * kw), axis=-1)
        p = p.reshape(B, H, GRID_W, kh, kw).astype(v.dtype)
        return jnp.einsum('bhcrk,brckhd->bchd', p, v_win)

    o = lax.map(one_row, (jnp.arange(rows), qg))
    return o.transpose(1, 0, 2, 3, 4).reshape(B, S, H * Dh)


def memory_cross_attention(h, mem_k, mem_v, w_q, q_g, w_o):
    B, S, _ = h.shape
    q = rms_norm((h @ w_q).reshape(B, S, MEM_HEADS, MEM_HEAD_DIM), q_g)
    s = jnp.einsum('bshd,bmhd->bhsm', q, mem_k).astype(jnp.float32) * (MEM_HEAD_DIM ** -0.5)
    p = jax.nn.softmax(s, axis=-1).astype(mem_v.dtype)
    o = jnp.einsum('bhsm,bmhd->bshd', p, mem_v).reshape(B, S, MEM_HEADS * MEM_HEAD_DIM)
    return o @ w_o


def squared_relu_mlp(h, w1, w2):
    a = jax.nn.relu(h @ w1)
    return (a * a) @ w2


def setup_inputs(seed: int = 0) -> dict:
    key = jax.random.key(seed)
    ks = iter(jax.random.split(key, 40))
    n_even = (DEPTH + 1) // 2
    n_odd = DEPTH // 2

    def w(shape, fan_in):
        return jax.random.normal(next(ks), shape, jnp.float32) * (fan_in ** -0.5)

    def gain(shape):
        return 1.0 + 0.02 * jax.random.normal(next(ks), shape, jnp.float32)

    return {
        "x": jax.random.normal(next(ks), (BATCH, SEQ, D_MODEL), jnp.float32),
        "mem": jax.random.normal(next(ks), (BATCH, N_MEM, D_MODEL), jnp.float32),
        "mix_norm_g": gain((DEPTH, D_MODEL)),
        "xattn_norm_g": gain((DEPTH, D_MODEL)),
        "ff_norm_g": gain((DEPTH, D_MODEL)),
        "w_mem_q": w((DEPTH, D_MODEL, MEM_HEADS * MEM_HEAD_DIM), D_MODEL),
        "mem_q_g": gain((DEPTH, MEM_HEAD_DIM)),
        "w_mem_o": w((DEPTH, MEM_HEADS * MEM_HEAD_DIM, D_MODEL), MEM_HEADS * MEM_HEAD_DIM),
        "w_ff1": w((DEPTH, D_MODEL, D_FF), D_MODEL),
        "w_ff2": w((DEPTH, D_FF, D_MODEL), D_FF),
        "mem_tok_norm_g": gain((D_MODEL,)),
        "w_mem_kv": w((D_MODEL, 2 * MEM_HEADS * MEM_HEAD_DIM), D_MODEL),
        "mem_k_g": gain((MEM_HEAD_DIM,)),
        "w_in_e": w((n_even, D_MODEL, W_IN_EVEN), D_MODEL),
        "pool_w": w((n_even, POOL_GROUPS, POOL_GROUP_W, POOL_GROUP_W), POOL_GROUP_W),
        "pool_scale": gain((n_even, POOL_WIDTH)),
        "q_lora_g": gain((n_even, Q_LORA)),
        "w_uq": w((n_even, Q_LORA, MLA_HEADS * MLA_QK), Q_LORA),
        "kv_lora_g": gain((n_even, KV_LORA)),
        "w_ukv": w((n_even, KV_LORA, MLA_HEADS * (MLA_NOPE + MLA_V)), KV_LORA),
        "mla_q_g": gain((n_even, MLA_QK)),
        "mla_k_g": gain((n_even, MLA_QK)),
        "w_out_e": w((n_even, MIX_EVEN, D_MODEL), MIX_EVEN),
        "w_qkv_o": w((n_odd, D_MODEL, 3 * NA_WIDTH), D_MODEL),
        "na_q_g": gain((n_odd, NA_HEAD_DIM)),
        "na_k_g": gain((n_odd, NA_HEAD_DIM)),
        "na_rpb": 0.1 * jax.random.normal(next(ks), (n_odd, NA_HEADS, 2 * NA_KH - 1, 2 * NA_KW - 1), jnp.float32),
        "w_out_o": w((n_odd, NA_WIDTH, D_MODEL), NA_WIDTH),
    }


def reference(x, mem, mix_norm_g, xattn_norm_g, ff_norm_g, w_mem_q, mem_q_g, w_mem_o,
              w_ff1, w_ff2, mem_tok_norm_g, w_mem_kv, mem_k_g, w_in_e, pool_w, pool_scale,
              q_lora_g, w_uq, kv_lora_g, w_ukv, mla_q_g, mla_k_g, w_out_e, w_qkv_o,
              na_q_g, na_k_g, na_rpb, w_out_o):
    B, S, _ = x.shape
    mkv = (rms_norm(mem, mem_tok_norm_g) @ w_mem_kv).reshape(B, mem.shape[1], 2, MEM_HEADS, MEM_HEAD_DIM)
    mem_k = rms_norm(mkv[:, :, 0], mem_k_g)
    mem_v = mkv[:, :, 1]

    for i in range(DEPTH):
        h = rms_norm(x, mix_norm_g[i])
        if i % 2 == 0:
            e = i // 2
            u = h @ w_in_e[e]
            o1 = POOL_WIDTH
            o2 = o1 + Q_LORA
            o3 = o2 + KV_LORA
            a = pool_mixer(u[..., :o1], pool_w[e], pool_scale[e])
            b = mla_mixer(u[..., o1:o2], u[..., o2:o3], u[..., o3:],
                          q_lora_g[e], w_uq[e], kv_lora_g[e], w_ukv[e], mla_q_g[e], mla_k_g[e])
            x = x + jnp.concatenate([a, b], axis=-1) @ w_out_e[e]
        else:
            o = i // 2
            qkv = (h @ w_qkv_o[o]).reshape(B, S, 3, NA_HEADS, NA_HEAD_DIM)
            q = rms_norm(qkv[:, :, 0], na_q_g[o])
            k = rms_norm(qkv[:, :, 1], na_k_g[o])
            v = qkv[:, :, 2]
            c = neighbourhood_attention(q, k, v, na_rpb[o])
            x = x + c @ w_out_o[o]
        x = x + memory_cross_attention(rms_norm(x, xattn_norm_g[i]), mem_k, mem_v,
                                       w_mem_q[i], mem_q_g[i], w_mem_o[i])
        x = x + squared_relu_mlp(rms_norm(x, ff_norm_g[i]), w_ff1[i], w_ff2[i])
    return x
```

```python
import functools

import jax
import jax.numpy as jnp
from jax import lax
from jax.experimental import pallas as pl
from jax.experimental.pallas import tpu as pltpu

F32 = jnp.float32
BF16 = jnp.bfloat16

D_MODEL = 1024
GRID_W = 64
EPS = 1e-6
POOL_WIDTH = 512
POOL_GROUP_W = 128
POOL_WINDOWS = (2, 4, 8, 16)
POOL_HALO = 8
MLA_HEADS = 8
MLA_NOPE = 64
MLA_ROPE = 32
MLA_V = 64
MLA_QK = MLA_NOPE + MLA_ROPE
Q_LORA = 256
KV_LORA = 128
ROPE_THETA = 10000.0
NA_HEADS = 16
NA_HEAD_DIM = 64
NA_KH = 8
NA_KW = 16
MEM_HEADS = 4
MEM_HEAD_DIM = 256
D_FF = 4096

LANES = 128
MASK_VALUE = -1e30
VMEM_LIMIT = 48 * 1024 * 1024

TOKEN_TILE = 512
MLA_Q_TILE = 512
MLA_KV_TILE = 512
NA_ROWS_PER_STEP = 8


def _const_spec(shape):
    zeros = (0,) * len(shape)
    return pl.BlockSpec(shape, lambda *_: zeros, pipeline_mode=pl.Buffered(1))


def _params():
    return pltpu.CompilerParams(vmem_limit_bytes=VMEM_LIMIT)


def _rms(x, g, n=None):
    n = x.shape[-1] if n is None else n
    ms = jnp.sum(x * x, axis=-1, keepdims=True) * (1.0 / n)
    return x * lax.rsqrt(ms + EPS) * g


def _dot(a, b):
    return jnp.dot(a, b, preferred_element_type=F32)


def _dot_nt(a, b):
    return lax.dot_general(a, b, (((1,), (1,)), ((), ())), preferred_element_type=F32)


def _mem_kv_kernel(mem_ref, g_ref, w_ref, kg_ref, k_out, v_out):
    mn = _rms(mem_ref[0], g_ref[...]).astype(BF16)
    kv = _dot(mn, w_ref[...])
    for h in range(MEM_HEADS):
        sl = slice(h * MEM_HEAD_DIM, (h + 1) * MEM_HEAD_DIM)
        k_out[0, :, sl] = _rms(kv[:, sl], kg_ref[...]).astype(BF16)
    v_out[0] = kv[:, D_MODEL:].astype(BF16)


def _mem_kv(mem, g, w_kv, k_g):
    B, N, D = mem.shape
    return pl.pallas_call(
        _mem_kv_kernel,
        grid=(B,),
        in_specs=[
            pl.BlockSpec((1, N, D), lambda b: (b, 0, 0)),
            _const_spec((1, D)),
            _const_spec((D, 2 * D)),
            _const_spec((1, MEM_HEAD_DIM)),
        ],
        out_specs=[pl.BlockSpec((1, N, D), lambda b: (b, 0, 0))] * 2,
        out_shape=[jax.ShapeDtypeStruct((B, N, D), BF16)] * 2,
        compiler_params=_params(),
        name="mem_kv",
    )(mem, g, w_kv, k_g)


def _rope(x, cos, sin_a, sin_b):
    return x * cos + pltpu.roll(x, LANES - 16, 1) * sin_a + pltpu.roll(x, 16, 1) * sin_b


def _l0_front_kernel(x_ref, g_ref, win_ref, qlg_ref, wuq_ref, kvlg_ref, wuk_ref, wuv_ref,
                     qg_ref, kg_ref, cos_ref, sa_ref, sb_ref,
                     up_ref, q_ref, k_ref, v_ref):
    h = _rms(x_ref[0], g_ref[...]).astype(BF16)
    u = _dot(h, win_ref[...])
    up_ref[0] = u[:, :POOL_WIDTH]
    o2 = POOL_WIDTH + Q_LORA
    o3 = o2 + KV_LORA
    cq = _rms(u[:, POOL_WIDTH:o2], qlg_ref[...]).astype(BF16)
    ckv = _rms(u[:, o2:o3], kvlg_ref[...]).astype(BF16)
    kr = u[:, o3:]
    qf = _dot(cq, wuq_ref[...])
    kf = _dot(ckv, wuk_ref[...])
    vf = _dot(ckv, wuv_ref[...])
    cos, sa, sb = cos_ref[...], sa_ref[...], sb_ref[...]
    lane = lax.broadcasted_iota(jnp.int32, (1, LANES), 1)
    scale = MLA_QK ** -0.5
    for hd in range(MLA_HEADS):
        sl = slice(hd * LANES, (hd + 1) * LANES)
        qh = _rope(_rms(qf[:, sl], qg_ref[...], n=MLA_QK), cos, sa, sb) * scale
        q_ref[0, hd] = qh.astype(BF16)
        kh = _rope(_rms(kf[:, sl] + kr, kg_ref[...], n=MLA_QK), cos, sa, sb)
        k_ref[0, hd] = kh.astype(BF16)
        v_ref[0, hd] = jnp.where(lane == MLA_V, 1.0, vf[:, sl]).astype(BF16)


def _l0_front(x, g, w_in, qlg, wuq, kvlg, wuk, wuv, qg, kg, cos, sa, sb):
    B, S, D = x.shape
    tm = TOKEN_TILE
    hw = MLA_HEADS * LANES
    head_spec = pl.BlockSpec((1, MLA_HEADS, tm, LANES), lambda b, i: (b, 0, i, 0))
    tab_spec = pl.BlockSpec((tm, LANES), lambda b, i: (i, 0))
    return pl.pallas_call(
        _l0_front_kernel,
        grid=(B, S // tm),
        in_specs=[
            pl.BlockSpec((1, tm, D), lambda b, i: (b, i, 0)),
            _const_spec((1, D)),
            _const_spec((D, D)),
            _const_spec((1, Q_LORA)),
            _const_spec((Q_LORA, hw)),
            _const_spec((1, KV_LORA)),
            _const_spec((KV_LORA, hw)),
            _const_spec((KV_LORA, hw)),
            _const_spec((1, LANES)),
            _const_spec((1, LANES)),
            tab_spec, tab_spec, tab_spec,
        ],
        out_specs=[
            pl.BlockSpec((1, tm, POOL_WIDTH), lambda b, i: (b, i, 0)),
            head_spec, head_spec, head_spec,
        ],
        out_shape=[
            jax.ShapeDtypeStruct((B, S, POOL_WIDTH), F32),
            jax.ShapeDtypeStruct((B, MLA_HEADS, S, LANES), BF16),
            jax.ShapeDtypeStruct((B, MLA_HEADS, S, LANES), BF16),
            jax.ShapeDtypeStruct((B, MLA_HEADS, S, LANES), BF16),
        ],
        compiler_params=_params(),
        name="l0_front",
    )(x, g, w_in, qlg, wuq, kvlg, wuk, wuv, qg, kg, cos, sa, sb)


def _mla_attn_kernel(q_ref, k_ref, v_ref, o_ref, *, tk):
    q = q_ref[0, 0]
    tq = q.shape[0]
    nk = k_ref.shape[2] // tk

    def body(j, carry):
        m, acc = carry
        start = pl.multiple_of(j * tk, tk)
        kc = k_ref[0, 0, pl.ds(start, tk), :]
        vc = v_ref[0, 0, pl.ds(start, tk), :]
        s = _dot_nt(q, kc)
        m_new = jnp.maximum(m, jnp.max(s, axis=-1, keepdims=True))
        alpha = jnp.exp(m - m_new)
        p = jnp.exp(s - m_new).astype(BF16)
        acc = acc * alpha + _dot(p, vc)
        return m_new, acc

    m0 = jnp.full((tq, 1), MASK_VALUE, F32)
    acc0 = jnp.zeros((tq, LANES), F32)
    _, acc = lax.fori_loop(0, nk, body, (m0, acc0))
    denom = acc[:, MLA_V:MLA_V + 1]
    o_ref[0] = (acc * (1.0 / denom)).astype(BF16)


def _mla_attn(q, k, v):
    B, H, S, L = q.shape
    tq = MLA_Q_TILE
    kv_spec = pl.BlockSpec((1, 1, S, L), lambda b, h, i: (b, h, 0, 0))
    return pl.pallas_call(
        functools.partial(_mla_attn_kernel, tk=MLA_KV_TILE),
        grid=(B, H, S // tq),
        in_specs=[pl.BlockSpec((1, 1, tq, L), lambda b, h, i: (b, h, i, 0)), kv_spec, kv_spec],
        out_specs=pl.BlockSpec((1, tq, L), lambda b, h, i: (b, i, h)),
        out_shape=jax.ShapeDtypeStruct((B, S, H * L), BF16),
        compiler_params=_params(),
        name="mla_attn",
    )(q, k, v)


def _pool_mix(up_ref, upp_ref, upn_ref, pw_ref, ps_ref, ext_ref, tm, seq_len):
    i = pl.program_id(1)
    last = pl.num_programs(1) - 1
    ext_ref[0:POOL_HALO] = jnp.where(i > 0, upp_ref[0], 0.0)
    ext_ref[POOL_HALO:POOL_HALO + tm] = up_ref[0]
    ext_ref[POOL_HALO + tm:2 * POOL_HALO + tm] = jnp.where(i < last, upn_ref[0], 0.0)
    t = i * tm + lax.broadcasted_iota(jnp.int32, (tm, 1), 0)
    outs = []
    for g, w in enumerate(POOL_WINDOWS):
        cols = slice(g * POOL_GROUP_W, (g + 1) * POOL_GROUP_W)
        wsum = None
        for o in range(-(w // 2), w - w // 2):
            piece = ext_ref[POOL_HALO + o:POOL_HALO + o + tm, cols]
            wsum = piece if wsum is None else wsum + piece
        lo = jnp.maximum(t - w // 2, 0)
        hi = jnp.minimum(t + (w - 1 - w // 2), seq_len - 1)
        cnt = (hi - lo + 1).astype(F32)
        d = wsum / cnt - ext_ref[POOL_HALO:POOL_HALO + tm, cols]
        outs.append(_dot(d.astype(BF16), pw_ref[g]))
    return jnp.concatenate(outs, axis=-1) * ps_ref[...]


def _mem_xattn(x1, xg_ref, wq_ref, qg_ref, mk_ref, mv_ref, wo_ref):
    h2 = _rms(x1, xg_ref[...]).astype(BF16)
    qm = _dot(h2, wq_ref[...])
    scale = MEM_HEAD_DIM ** -0.5
    outs = []
    for hd in range(MEM_HEADS):
        sl = slice(hd * MEM_HEAD_DIM, (hd + 1) * MEM_HEAD_DIM)
        qh = (_rms(qm[:, sl], qg_ref[...]) * scale).astype(BF16)
        s = _dot_nt(qh, mk_ref[0, :, sl])
        e = jnp.exp(s - jnp.max(s, axis=-1, keepdims=True))
        denom = jnp.sum(e, axis=-1, keepdims=True)
        oh = _dot(e.astype(BF16), mv_ref[0, :, sl]) * (1.0 / denom)
        outs.append(oh.astype(BF16))
    return x1 + _dot(jnp.concatenate(outs, axis=-1), wo_ref[...])


def _l0_out_kernel(x_ref, up_ref, upp_ref, upn_ref, pw_ref, ps_ref, wa_ref, o_ref, wb_ref,
                   xg_ref, wq_ref, qg_ref, mk_ref, mv_ref, wo_ref, out_ref, ext_ref, *, tm, seq_len):
    a = _pool_mix(up_ref, upp_ref, upn_ref, pw_ref, ps_ref, ext_ref, tm, seq_len)
    x1 = x_ref[0] + _dot(a.astype(BF16), wa_ref[...]) + _dot(o_ref[0], wb_ref[...])
    out_ref[0] = _mem_xattn(x1, xg_ref, wq_ref, qg_ref, mk_ref, mv_ref, wo_ref)


def _l1_out_kernel(x_ref, o_ref, wb_ref, xg_ref, wq_ref, qg_ref, mk_ref, mv_ref, wo_ref, out_ref):
    x1 = x_ref[0] + _dot(o_ref[0], wb_ref[...])
    out_ref[0] = _mem_xattn(x1, xg_ref, wq_ref, qg_ref, mk_ref, mv_ref, wo_ref)


def _xattn_specs(n_mem):
    D = D_MODEL
    return [
        _const_spec((1, D)),
        _const_spec((D, D)),
        _const_spec((1, MEM_HEAD_DIM)),
        pl.BlockSpec((1, n_mem, D), lambda b, i: (b, 0, 0)),
        pl.BlockSpec((1, n_mem, D), lambda b, i: (b, 0, 0)),
        _const_spec((D, D)),
    ]


def _l0_out(x, up, pool_w, pool_scale, wa, o, wb, xg, wq, qg, mk, mv, wo):
    B, S, D = x.shape
    tm = TOKEN_TILE
    halo_blocks = tm // POOL_HALO
    tile = pl.BlockSpec((1, tm, D), lambda b, i: (b, i, 0))
    return pl.pallas_call(
        functools.partial(_l0_out_kernel, tm=tm, seq_len=S),
        grid=(B, S // tm),
        in_specs=[
            tile,
            pl.BlockSpec((1, tm, POOL_WIDTH), lambda b, i: (b, i, 0)),
            pl.BlockSpec((1, POOL_HALO, POOL_WIDTH),
                         lambda b, i: (b, jnp.maximum(i * halo_blocks - 1, 0), 0)),
            pl.BlockSpec((1, POOL_HALO, POOL_WIDTH),
                         lambda b, i: (b, jnp.minimum((i + 1) * halo_blocks, S // POOL_HALO - 1), 0)),
            _const_spec(pool_w.shape),
            _const_spec((1, POOL_WIDTH)),
            _const_spec((POOL_WIDTH, D)),
            pl.BlockSpec((1, tm, o.shape[-1]), lambda b, i: (b, i, 0)),
            _const_spec(wb.shape),
        ] + _xattn_specs(mk.shape[1]),
        out_specs=tile,
        out_shape=jax.ShapeDtypeStruct((B, S, D), F32),
        scratch_shapes=[pltpu.VMEM((tm + 2 * POOL_HALO, POOL_WIDTH), F32)],
        compiler_params=_params(),
        name="l0_out_xattn",
    )(x, up, up, up, pool_w, pool_scale, wa, o, wb, xg, wq, qg, mk, mv, wo)


def _l1_out(x, o, wb, xg, wq, qg, mk, mv, wo):
    B, S, D = x.shape
    tm = TOKEN_TILE
    tile = pl.BlockSpec((1, tm, D), lambda b, i: (b, i, 0))
    return pl.pallas_call(
        _l1_out_kernel,
        grid=(B, S // tm),
        in_specs=[tile, pl.BlockSpec((1, tm, o.shape[-1]), lambda b, i: (b, i, 0)), _const_spec(wb.shape)]
        + _xattn_specs(mk.shape[1]),
        out_specs=tile,
        out_shape=jax.ShapeDtypeStruct((B, S, D), F32),
        compiler_params=_params(),
        name="l1_out_xattn",
    )(x, o, wb, xg, wq, qg, mk, mv, wo)


def _mlp_kernel(x_ref, g_ref, w1_ref, w2_ref, out_ref, *, ff_chunk):
    x = x_ref[0]
    h = _rms(x, g_ref[...]).astype(BF16)
    acc = x
    for c in range(D_FF // ff_chunk):
        sl = slice(c * ff_chunk, (c + 1) * ff_chunk)
        a = jnp.maximum(_dot(h, w1_ref[:, sl]), 0.0)
        acc = acc + _dot((a * a).astype(BF16), w2_ref[sl, :])
    out_ref[0] = acc


def _mlp(x, g, w1, w2):
    B, S, D = x.shape
    tm = TOKEN_TILE
    tile = pl.BlockSpec((1, tm, D), lambda b, i: (b, i, 0))
    return pl.pallas_call(
        functools.partial(_mlp_kernel, ff_chunk=1024),
        grid=(B, S // tm),
        in_specs=[tile, _const_spec((1, D)), _const_spec((D, D_FF)), _const_spec((D_FF, D))],
        out_specs=tile,
        out_shape=jax.ShapeDtypeStruct((B, S, D), F32),
        compiler_params=_params(),
        name="mlp",
    )(x, g, w1, w2)


def _pair_norm(z, g2, lo_mask):
    y = z * z
    lo = jnp.sum(jnp.where(lo_mask, y, 0.0), axis=-1, keepdims=True)
    hi = jnp.sum(jnp.where(lo_mask, 0.0, y), axis=-1, keepdims=True)
    inv = 1.0 / NA_HEAD_DIM
    r = jnp.where(lo_mask, lax.rsqrt(lo * inv + EPS), lax.rsqrt(hi * inv + EPS))
    return z * r * g2


def _l1_front_kernel(x_ref, g_ref, w_ref, qg_ref, kg_ref, q_out, k_out, v_out):
    h = _rms(x_ref[0], g_ref[...]).astype(BF16)
    qkv = _dot(h, w_ref[...])
    width = NA_HEADS * NA_HEAD_DIM
    lo_mask = lax.broadcasted_iota(jnp.int32, (1, LANES), 1) < NA_HEAD_DIM
    scale = NA_HEAD_DIM ** -0.5
    for j in range(width // LANES):
        sl = slice(j * LANES, (j + 1) * LANES)
        q_out[0, :, sl] = (_pair_norm(qkv[:, sl], qg_ref[...], lo_mask) * scale).astype(BF16)
        ksl = slice(width + j * LANES, width + (j + 1) * LANES)
        k_out[0, :, sl] = _pair_norm(qkv[:, ksl], kg_ref[...], lo_mask).astype(BF16)
    v_out[0] = qkv[:, 2 * width:].astype(BF16)


def _l1_front(x, g, w_qkv, qg2, kg2):
    B, S, D = x.shape
    tm = TOKEN_TILE
    tile = pl.BlockSpec((1, tm, D), lambda b, i: (b, i, 0))
    return pl.pallas_call(
        _l1_front_kernel,
        grid=(B, S // tm),
        in_specs=[tile, _const_spec((1, D)), _const_spec((D, 3 * D)), _const_spec((1, LANES)),
                  _const_spec((1, LANES))],
        out_specs=[tile] * 3,
        out_shape=[jax.ShapeDtypeStruct((B, S, D), BF16)] * 3,
        compiler_params=_params(),
        name="l1_front",
    )(x, g, w_qkv, qg2, kg2)


def _na_kernel(q_ref, k_ref, v_ref, bias_ref, o_ref, *, rows_per_step, n_rows):
    i = pl.program_id(2)
    win = NA_KH * GRID_W
    lo_mask = lax.broadcasted_iota(jnp.int32, (1, LANES), 1) < NA_HEAD_DIM
    zero = jnp.zeros((), BF16)
    for rr in range(rows_per_step):
        r = i * rows_per_step + rr
        r0 = jnp.clip(r - NA_KH // 2, 0, n_rows - NA_KH)
        delta = r - r0
        start = pl.multiple_of(r0 * GRID_W, GRID_W)
        q_row = q_ref[0, rr * GRID_W:(rr + 1) * GRID_W, :]
        q2 = jnp.concatenate([jnp.where(lo_mask, q_row, zero), jnp.where(lo_mask, zero, q_row)], axis=0)
        k_win = k_ref[0, pl.ds(start, win), :]
        v_win = v_ref[0, pl.ds(start, win), :]
        bias = jnp.concatenate([bias_ref[0, delta], bias_ref[1, delta]], axis=0)
        s = _dot_nt(q2, k_win) + bias
        e = jnp.exp(s - jnp.max(s, axis=-1, keepdims=True))
        denom = jnp.sum(e, axis=-1, keepdims=True)
        o2 = _dot(e.astype(BF16), v_win) * (1.0 / denom)
        out = jnp.where(lo_mask, o2[:GRID_W], o2[GRID_W:])
        o_ref[0, rr * GRID_W:(rr + 1) * GRID_W, :] = out.astype(BF16)


def _na_attn(q, k, v, bias):
    B, S, D = q.shape
    n_rows = S // GRID_W
    tq = NA_ROWS_PER_STEP * GRID_W
    kv_spec = pl.BlockSpec((1, S, LANES), lambda b, j, i: (b, 0, j))
    return pl.pallas_call(
        functools.partial(_na_kernel, rows_per_step=NA_ROWS_PER_STEP, n_rows=n_rows),
        grid=(B, D // LANES, S // tq),
        in_specs=[
            pl.BlockSpec((1, tq, LANES), lambda b, j, i: (b, i, j)),
            kv_spec, kv_spec,
            pl.BlockSpec((2, NA_KH, GRID_W, NA_KH * GRID_W), lambda b, j, i: (j, 0, 0, 0)),
        ],
        out_specs=pl.BlockSpec((1, tq, LANES), lambda b, j, i: (b, i, j)),
        out_shape=jax.ShapeDtypeStruct((B, S, D), BF16),
        compiler_params=_params(),
        name="na_attn",
    )(q, k, v, bias)


def _na_bias_table(rpb):
    cols = jnp.arange(GRID_W)
    c0 = jnp.clip(cols - NA_KW // 2, 0, GRID_W - NA_KW)
    kc = jnp.arange(GRID_W)
    valid = (kc[None, :] >= c0[:, None]) & (kc[None, :] < c0[:, None] + NA_KW)
    dc = jnp.clip(kc[None, :] - cols[:, None] + (NA_KW - 1), 0, 2 * NA_KW - 2)
    t = jnp.where(valid[None, None], rpb[:, :, dc], MASK_VALUE)
    per_delta = []
    for delta in range(NA_KH):
        rows = t[:, NA_KH - 1 - delta:2 * NA_KH - 1 - delta]
        per_delta.append(rows.transpose(0, 2, 1, 3).reshape(rpb.shape[0], GRID_W, NA_KH * GRID_W))
    return jnp.stack(per_delta, axis=1)


def _pad_heads(w, heads, width):
    lead = w.shape[:-1]
    w = w.reshape(*lead, heads, width)
    w = jnp.pad(w, [(0, 0)] * len(lead) + [(0, 0), (0, LANES - width)])
    return w.reshape(*lead, heads * LANES)


def _rope_tables(seq_len):
    half = MLA_ROPE // 2
    pos = jnp.arange(seq_len, dtype=F32)
    freqs = ROPE_THETA ** (-jnp.arange(half, dtype=F32) / half)
    ang = pos[:, None] * freqs[None, :]
    cos, sin = jnp.cos(ang), jnp.sin(ang)
    z = lambda n: jnp.zeros((seq_len, n), F32)
    cos_t = jnp.concatenate([jnp.ones((seq_len, MLA_NOPE), F32), cos, cos, z(LANES - MLA_QK)], axis=1)
    sin_a = jnp.concatenate([z(MLA_NOPE), -sin, z(half), z(LANES - MLA_QK)], axis=1)
    sin_b = jnp.concatenate([z(MLA_NOPE), z(half), sin, z(LANES - MLA_QK)], axis=1)
    return cos_t, sin_a, sin_b


def kernel(x, mem, mix_norm_g, xattn_norm_g, ff_norm_g, w_mem_q, mem_q_g, w_mem_o, w_ff1, w_ff2, mem_tok_norm_g, w_mem_kv, mem_k_g, w_in_e, pool_w, pool_scale, q_lora_g, w_uq, kv_lora_g, w_ukv, mla_q_g, mla_k_g, w_out_e, w_qkv_o, na_q_g, na_k_g, na_rpb, w_out_o):
    B, S, D = x.shape
    row = lambda v: v.reshape(1, -1)

    mem_k, mem_v = _mem_kv(mem, row(mem_tok_norm_g), w_mem_kv.astype(BF16), row(mem_k_g))

    o1 = POOL_WIDTH
    o3 = o1 + Q_LORA + KV_LORA
    w_in = w_in_e[0]
    w_in_pad = jnp.concatenate(
        [w_in[:, :o3], jnp.zeros((D, MLA_NOPE), F32), w_in[:, o3:], jnp.zeros((D, LANES - MLA_QK), F32)],
        axis=1).astype(BF16)
    wuq = _pad_heads(w_uq[0], MLA_HEADS, MLA_QK).astype(BF16)
    w_ukv_h = w_ukv[0].reshape(KV_LORA, MLA_HEADS, MLA_NOPE + MLA_V)
    wuk = _pad_heads(w_ukv_h[:, :, :MLA_NOPE].reshape(KV_LORA, -1), MLA_HEADS, MLA_NOPE).astype(BF16)
    wuv = _pad_heads(w_ukv_h[:, :, MLA_NOPE:].reshape(KV_LORA, -1), MLA_HEADS, MLA_V).astype(BF16)
    qg = jnp.pad(mla_q_g[0], (0, LANES - MLA_QK)).reshape(1, LANES)
    kg = jnp.pad(mla_k_g[0], (0, LANES - MLA_QK)).reshape(1, LANES)
    cos_t, sin_a, sin_b = _rope_tables(S)
    up, q, k, v = _l0_front(x, row(mix_norm_g[0]), w_in_pad, row(q_lora_g[0]), wuq, row(kv_lora_g[0]),
                            wuk, wuv, qg, kg, cos_t, sin_a, sin_b)
    o = _mla_attn(q, k, v)
    w_out = w_out_e[0]
    wa = w_out[:POOL_WIDTH].astype(BF16)
    wb = jnp.pad(w_out[POOL_WIDTH:].reshape(MLA_HEADS, MLA_V, D), ((0, 0), (0, LANES - MLA_V), (0, 0)))
    wb = wb.reshape(MLA_HEADS * LANES, D).astype(BF16)
    x = _l0_out(x, up, pool_w[0].astype(BF16), row(pool_scale[0]), wa, o, wb,
                row(xattn_norm_g[0]), w_mem_q[0].astype(BF16), row(mem_q_g[0]), mem_k, mem_v,
                w_mem_o[0].astype(BF16))
    x = _mlp(x, row(ff_norm_g[0]), w_ff1[0].astype(BF16), w_ff2[0].astype(BF16))

    qg2 = jnp.tile(na_q_g[0], LANES // NA_HEAD_DIM).reshape(1, LANES)
    kg2 = jnp.tile(na_k_g[0], LANES // NA_HEAD_DIM).reshape(1, LANES)
    q, k, v = _l1_front(x, row(mix_norm_g[1]), w_qkv_o[0].astype(BF16), qg2, kg2)
    c = _na_attn(q, k, v, _na_bias_table(na_rpb[0]))
    x = _l1_out(x, c, w_out_o[0].astype(BF16), row(xattn_norm_g[1]), w_mem_q[1].astype(BF16),
                row(mem_q_g[1]), mem_k, mem_v, w_mem_o[1].astype(BF16))
    x = _mlp(x, row(ff_norm_g[1]), w_ff1[1].astype(BF16), w_ff2[1].astype(BF16))
    return x
```

```python
import functools

import jax
import jax.numpy as jnp
from jax import lax
from jax.experimental import pallas as pl
from jax.experimental.pallas import tpu as pltpu

F32 = jnp.float32
BF16 = jnp.bfloat16

D_MODEL = 1024
GRID_W = 64
EPS = 1e-6
POOL_WIDTH = 512
POOL_GROUP_W = 128
POOL_WINDOWS = (2, 4, 8, 16)
POOL_HALO = 8
MLA_HEADS = 8
MLA_NOPE = 64
MLA_ROPE = 32
MLA_V = 64
MLA_QK = MLA_NOPE + MLA_ROPE
Q_LORA = 256
KV_LORA = 128
ROPE_THETA = 10000.0
NA_HEADS = 16
NA_HEAD_DIM = 64
NA_KH = 8
NA_KW = 16
MEM_HEADS = 4
MEM_HEAD_DIM = 256
D_FF = 4096

LANES = 128
MASK_VALUE = -1e30
LOG2_E = 1.4426950408889634
VMEM_LIMIT = 48 * 1024 * 1024

TOKEN_TILE = 512
MLA_Q_TILE = 512
MLA_KV_TILE = 512
NA_ROWS_PER_STEP = 8


def _const_spec(shape):
    zeros = (0,) * len(shape)
    return pl.BlockSpec(shape, lambda *_: zeros, pipeline_mode=pl.Buffered(1))


def _params():
    return pltpu.CompilerParams(vmem_limit_bytes=VMEM_LIMIT)


def _rms(x, g, n=None):
    n = x.shape[-1] if n is None else n
    ms = jnp.sum(x * x, axis=-1, keepdims=True) * (1.0 / n)
    return x * lax.rsqrt(ms + EPS) * g


def _dot(a, b):
    return jnp.dot(a, b, preferred_element_type=F32)


def _dot_nt(a, b):
    return lax.dot_general(a, b, (((1,), (1,)), ((), ())), preferred_element_type=F32)


def _mem_kv_kernel(mem_ref, g_ref, w_ref, kg_ref, k_out, v_out):
    mn = _rms(mem_ref[0], g_ref[...]).astype(BF16)
    kv = _dot(mn, w_ref[...])
    for h in range(MEM_HEADS):
        sl = slice(h * MEM_HEAD_DIM, (h + 1) * MEM_HEAD_DIM)
        k_out[0, :, sl] = _rms(kv[:, sl], kg_ref[...]).astype(BF16)
    v_out[0] = kv[:, D_MODEL:].astype(BF16)


def _mem_kv(mem, g, w_kv, k_g):
    B, N, D = mem.shape
    return pl.pallas_call(
        _mem_kv_kernel,
        grid=(B,),
        in_specs=[
            pl.BlockSpec((1, N, D), lambda b: (b, 0, 0)),
            _const_spec((1, D)),
            _const_spec((D, 2 * D)),
            _const_spec((1, MEM_HEAD_DIM)),
        ],
        out_specs=[pl.BlockSpec((1, N, D), lambda b: (b, 0, 0))] * 2,
        out_shape=[jax.ShapeDtypeStruct((B, N, D), BF16)] * 2,
        compiler_params=_params(),
        name="mem_kv",
    )(mem, g, w_kv, k_g)


def _rope(x, cos, sin_a, sin_b):
    return x * cos + pltpu.roll(x, LANES - 16, 1) * sin_a + pltpu.roll(x, 16, 1) * sin_b


def _l0_front_kernel(x_ref, g_ref, win_ref, qlg_ref, wuq_ref, kvlg_ref, wuk_ref, wuv_ref,
                     qg_ref, kg_ref, cos_ref, sa_ref, sb_ref,
                     up_ref, q_ref, k_ref, v_ref):
    h = _rms(x_ref[0], g_ref[...]).astype(BF16)
    u = _dot(h, win_ref[...])
    up_ref[0] = u[:, :POOL_WIDTH]
    o2 = POOL_WIDTH + Q_LORA
    o3 = o2 + KV_LORA
    cq = _rms(u[:, POOL_WIDTH:o2], qlg_ref[...]).astype(BF16)
    ckv = _rms(u[:, o2:o3], kvlg_ref[...]).astype(BF16)
    kr = u[:, o3:]
    qf = _dot(cq, wuq_ref[...])
    kf = _dot(ckv, wuk_ref[...])
    vf = _dot(ckv, wuv_ref[...])
    cos, sa, sb = cos_ref[...], sa_ref[...], sb_ref[...]
    lane = lax.broadcasted_iota(jnp.int32, (1, LANES), 1)
    scale = MLA_QK ** -0.5 * LOG2_E
    for hd in range(MLA_HEADS):
        sl = slice(hd * LANES, (hd + 1) * LANES)
        qh = _rope(_rms(qf[:, sl], qg_ref[...], n=MLA_QK), cos, sa, sb) * scale
        q_ref[0, hd] = qh.T.astype(BF16)
        kh = _rope(_rms(kf[:, sl] + kr, kg_ref[...], n=MLA_QK), cos, sa, sb)
        k_ref[0, hd] = kh.astype(BF16)
        v_ref[0, hd] = jnp.where(lane == MLA_V, 1.0, vf[:, sl]).T.astype(BF16)


def _l0_front(x, g, w_in, qlg, wuq, kvlg, wuk, wuv, qg, kg, cos, sa, sb):
    B, S, D = x.shape
    tm = TOKEN_TILE
    hw = MLA_HEADS * LANES
    head_spec = pl.BlockSpec((1, MLA_HEADS, tm, LANES), lambda b, i: (b, 0, i, 0))
    head_t_spec = pl.BlockSpec((1, MLA_HEADS, LANES, tm), lambda b, i: (b, 0, 0, i))
    tab_spec = pl.BlockSpec((tm, LANES), lambda b, i: (i, 0))
    return pl.pallas_call(
        _l0_front_kernel,
        grid=(B, S // tm),
        in_specs=[
            pl.BlockSpec((1, tm, D), lambda b, i: (b, i, 0)),
            _const_spec((1, D)),
            _const_spec((D, D)),
            _const_spec((1, Q_LORA)),
            _const_spec((Q_LORA, hw)),
            _const_spec((1, KV_LORA)),
            _const_spec((KV_LORA, hw)),
            _const_spec((KV_LORA, hw)),
            _const_spec((1, LANES)),
            _const_spec((1, LANES)),
            tab_spec, tab_spec, tab_spec,
        ],
        out_specs=[
            pl.BlockSpec((1, tm, POOL_WIDTH), lambda b, i: (b, i, 0)),
            head_t_spec, head_spec, head_t_spec,
        ],
        out_shape=[
            jax.ShapeDtypeStruct((B, S, POOL_WIDTH), F32),
            jax.ShapeDtypeStruct((B, MLA_HEADS, LANES, S), BF16),
            jax.ShapeDtypeStruct((B, MLA_HEADS, S, LANES), BF16),
            jax.ShapeDtypeStruct((B, MLA_HEADS, LANES, S), BF16),
        ],
        compiler_params=_params(),
        name="l0_front",
    )(x, g, w_in, qlg, wuq, kvlg, wuk, wuv, qg, kg, cos, sa, sb)


def _mla_attn_kernel(qt_ref, k_ref, vt_ref, o_ref, sa_ref, sb_ref, m_ref, acc_ref, *, tk):
    qt = qt_ref[0, 0]
    tq = qt.shape[1]
    nk = k_ref.shape[2] // tk

    def scores(c):
        start = pl.multiple_of(c * tk, tk)
        return _dot(k_ref[0, 0, pl.ds(start, tk), :], qt)

    def update(s_ref, c):
        start = pl.multiple_of(c * tk, tk)
        vt = vt_ref[0, 0, :, pl.ds(start, tk)]
        m_old = m_ref[...]
        m_new = jnp.maximum(m_old, jnp.max(s_ref[...], axis=0, keepdims=True))
        alpha = jnp.exp2(m_old - m_new)
        p = jnp.exp2(s_ref[...] - m_new).astype(BF16)
        acc_ref[...] = acc_ref[...] * alpha + _dot(vt, p)
        m_ref[...] = m_new

    m_ref[...] = jnp.full(m_ref.shape, MASK_VALUE, F32)
    acc_ref[...] = jnp.zeros(acc_ref.shape, F32)
    sa_ref[...] = scores(0)

    def body(j, carry):
        c = 2 * j
        sb_ref[...] = scores(c + 1)
        update(sa_ref, c)
        sa_ref[...] = scores(jnp.minimum(c + 2, nk - 1))
        update(sb_ref, c + 1)
        return carry

    lax.fori_loop(0, nk // 2, body, 0)
    acc = acc_ref[...]
    denom = acc[MLA_V:MLA_V + 1, :]
    o_ref[0] = (acc * (1.0 / denom)).T.astype(BF16)


def _mla_attn(qt, k, vt):
    B, H, S, L = k.shape
    tq, tk = MLA_Q_TILE, MLA_KV_TILE
    assert S % (2 * tk) == 0 and S % tq == 0
    return pl.pallas_call(
        functools.partial(_mla_attn_kernel, tk=tk),
        grid=(B, H, S // tq),
        in_specs=[
            pl.BlockSpec((1, 1, L, tq), lambda b, h, i: (b, h, 0, i)),
            pl.BlockSpec((1, 1, S, L), lambda b, h, i: (b, h, 0, 0)),
            pl.BlockSpec((1, 1, L, S), lambda b, h, i: (b, h, 0, 0)),
        ],
        out_specs=pl.BlockSpec((1, tq, L), lambda b, h, i: (b, i, h)),
        out_shape=jax.ShapeDtypeStruct((B, S, H * L), BF16),
        scratch_shapes=[
            pltpu.VMEM((tk, tq), F32),
            pltpu.VMEM((tk, tq), F32),
            pltpu.VMEM((1, tq), F32),
            pltpu.VMEM((L, tq), F32),
        ],
        compiler_params=_params(),
        name="mla_attn",
    )(qt, k, vt)


def _pool_mix(up_ref, upp_ref, upn_ref, pw_ref, ps_ref, ext_ref, tm, seq_len):
    i = pl.program_id(1)
    last = pl.num_programs(1) - 1
    ext_ref[0:POOL_HALO] = jnp.where(i > 0, upp_ref[0], 0.0)
    ext_ref[POOL_HALO:POOL_HALO + tm] = up_ref[0]
    ext_ref[POOL_HALO + tm:2 * POOL_HALO + tm] = jnp.where(i < last, upn_ref[0], 0.0)
    t = i * tm + lax.broadcasted_iota(jnp.int32, (tm, 1), 0)
    outs = []
    for g, w in enumerate(POOL_WINDOWS):
        cols = slice(g * POOL_GROUP_W, (g + 1) * POOL_GROUP_W)
        wsum = None
        for o in range(-(w // 2), w - w // 2):
            piece = ext_ref[POOL_HALO + o:POOL_HALO + o + tm, cols]
            wsum = piece if wsum is None else wsum + piece
        lo = jnp.maximum(t - w // 2, 0)
        hi = jnp.minimum(t + (w - 1 - w // 2), seq_len - 1)
        cnt = (hi - lo + 1).astype(F32)
        d = wsum / cnt - ext_ref[POOL_HALO:POOL_HALO + tm, cols]
        outs.append(_dot(d.astype(BF16), pw_ref[g]))
    return jnp.concatenate(outs, axis=-1) * ps_ref[...]


def _mem_xattn(x1, xg_ref, wq_ref, qg_ref, mk_ref, mv_ref, wo_ref):
    h2 = _rms(x1, xg_ref[...]).astype(BF16)
    qm = _dot(h2, wq_ref[...])
    scale = MEM_HEAD_DIM ** -0.5
    outs = []
    for hd in range(MEM_HEADS):
        sl = slice(hd * MEM_HEAD_DIM, (hd + 1) * MEM_HEAD_DIM)
        qh = (_rms(qm[:, sl], qg_ref[...]) * scale).astype(BF16)
        s = _dot_nt(qh, mk_ref[0, :, sl])
        e = jnp.exp(s - jnp.max(s, axis=-1, keepdims=True))
        denom = jnp.sum(e, axis=-1, keepdims=True)
        oh = _dot(e.astype(BF16), mv_ref[0, :, sl]) * (1.0 / denom)
        outs.append(oh.astype(BF16))
    return x1 + _dot(jnp.concatenate(outs, axis=-1), wo_ref[...])


def _l0_out_kernel(x_ref, up_ref, upp_ref, upn_ref, pw_ref, ps_ref, wa_ref, o_ref, wb_ref,
                   xg_ref, wq_ref, qg_ref, mk_ref, mv_ref, wo_ref, out_ref, ext_ref, *, tm, seq_len):
    a = _pool_mix(up_ref, upp_ref, upn_ref, pw_ref, ps_ref, ext_ref, tm, seq_len)
    x1 = x_ref[0] + _dot(a.astype(BF16), wa_ref[...]) + _dot(o_ref[0], wb_ref[...])
    out_ref[0] = _mem_xattn(x1, xg_ref, wq_ref, qg_ref, mk_ref, mv_ref, wo_ref)


def _l1_out_kernel(x_ref, o_ref, wb_ref, xg_ref, wq_ref, qg_ref, mk_ref, mv_ref, wo_ref, out_ref):
    x1 = x_ref[0] + _dot(o_ref[0], wb_ref[...])
    out_ref[0] = _mem_xattn(x1, xg_ref, wq_ref, qg_ref, mk_ref, mv_ref, wo_ref)


def _xattn_specs(n_mem):
    D = D_MODEL
    return [
        _const_spec((1, D)),
        _const_spec((D, D)),
        _const_spec((1, MEM_HEAD_DIM)),
        pl.BlockSpec((1, n_mem, D), lambda b, i: (b, 0, 0)),
        pl.BlockSpec((1, n_mem, D), lambda b, i: (b, 0, 0)),
        _const_spec((D, D)),
    ]


def _l0_out(x, up, pool_w, pool_scale, wa, o, wb, xg, wq, qg, mk, mv, wo):
    B, S, D = x.shape
    tm = TOKEN_TILE
    halo_blocks = tm // POOL_HALO
    tile = pl.BlockSpec((1, tm, D), lambda b, i: (b, i, 0))
    return pl.pallas_call(
        functools.partial(_l0_out_kernel, tm=tm, seq_len=S),
        grid=(B, S // tm),
        in_specs=[
            tile,
            pl.BlockSpec((1, tm, POOL_WIDTH), lambda b, i: (b, i, 0)),
            pl.BlockSpec((1, POOL_HALO, POOL_WIDTH),
                         lambda b, i: (b, jnp.maximum(i * halo_blocks - 1, 0), 0)),
            pl.BlockSpec((1, POOL_HALO, POOL_WIDTH),
                         lambda b, i: (b, jnp.minimum((i + 1) * halo_blocks, S // POOL_HALO - 1), 0)),
            _const_spec(pool_w.shape),
            _const_spec((1, POOL_WIDTH)),
            _const_spec((POOL_WIDTH, D)),
            pl.BlockSpec((1, tm, o.shape[-1]), lambda b, i: (b, i, 0)),
            _const_spec(wb.shape),
        ] + _xattn_specs(mk.shape[1]),
        out_specs=tile,
        out_shape=jax.ShapeDtypeStruct((B, S, D), F32),
        scratch_shapes=[pltpu.VMEM((tm + 2 * POOL_HALO, POOL_WIDTH), F32)],
        compiler_params=_params(),
        name="l0_out_xattn",
    )(x, up, up, up, pool_w, pool_scale, wa, o, wb, xg, wq, qg, mk, mv, wo)


def _l1_out(x, o, wb, xg, wq, qg, mk, mv, wo):
    B, S, D = x.shape
    tm = TOKEN_TILE
    tile = pl.BlockSpec((1, tm, D), lambda b, i: (b, i, 0))
    return pl.pallas_call(
        _l1_out_kernel,
        grid=(B, S // tm),
        in_specs=[tile, pl.BlockSpec((1, tm, o.shape[-1]), lambda b, i: (b, i, 0)), _const_spec(wb.shape)]
        + _xattn_specs(mk.shape[1]),
        out_specs=tile,
        out_shape=jax.ShapeDtypeStruct((B, S, D), F32),
        compiler_params=_params(),
        name="l1_out_xattn",
    )(x, o, wb, xg, wq, qg, mk, mv, wo)


def _mlp_kernel(x_ref, g_ref, w1_ref, w2_ref, out_ref, *, ff_chunk):
    x = x_ref[0]
    h = _rms(x, g_ref[...]).astype(BF16)
    acc = x
    for c in range(D_FF // ff_chunk):
        sl = slice(c * ff_chunk, (c + 1) * ff_chunk)
        a = jnp.maximum(_dot(h, w1_ref[:, sl]), 0.0)
        acc = acc + _dot((a * a).astype(BF16), w2_ref[sl, :])
    out_ref[0] = acc


def _mlp(x, g, w1, w2):
    B, S, D = x.shape
    tm = TOKEN_TILE
    tile = pl.BlockSpec((1, tm, D), lambda b, i: (b, i, 0))
    return pl.pallas_call(
        functools.partial(_mlp_kernel, ff_chunk=1024),
        grid=(B, S // tm),
        in_specs=[tile, _const_spec((1, D)), _const_spec((D, D_FF)), _const_spec((D_FF, D))],
        out_specs=tile,
        out_shape=jax.ShapeDtypeStruct((B, S, D), F32),
        compiler_params=_params(),
        name="mlp",
    )(x, g, w1, w2)


def _pair_norm(z, g2, lo_mask):
    y = z * z
    lo = jnp.sum(jnp.where(lo_mask, y, 0.0), axis=-1, keepdims=True)
    hi = jnp.sum(jnp.where(lo_mask, 0.0, y), axis=-1, keepdims=True)
    inv = 1.0 / NA_HEAD_DIM
    r = jnp.where(lo_mask, lax.rsqrt(lo * inv + EPS), lax.rsqrt(hi * inv + EPS))
    return z * r * g2


def _l1_front_kernel(x_ref, g_ref, w_ref, qg_ref, kg_ref, q_out, k_out, v_out):
    h = _rms(x_ref[0], g_ref[...]).astype(BF16)
    qkv = _dot(h, w_ref[...])
    width = NA_HEADS * NA_HEAD_DIM
    lo_mask = lax.broadcasted_iota(jnp.int32, (1, LANES), 1) < NA_HEAD_DIM
    scale = NA_HEAD_DIM ** -0.5
    for j in range(width // LANES):
        sl = slice(j * LANES, (j + 1) * LANES)
        q_out[0, :, sl] = (_pair_norm(qkv[:, sl], qg_ref[...], lo_mask) * scale).astype(BF16)
        ksl = slice(width + j * LANES, width + (j + 1) * LANES)
        k_out[0, :, sl] = _pair_norm(qkv[:, ksl], kg_ref[...], lo_mask).astype(BF16)
    v_out[0] = qkv[:, 2 * width:].astype(BF16)


def _l1_front(x, g, w_qkv, qg2, kg2):
    B, S, D = x.shape
    tm = TOKEN_TILE
    tile = pl.BlockSpec((1, tm, D), lambda b, i: (b, i, 0))
    return pl.pallas_call(
        _l1_front_kernel,
        grid=(B, S // tm),
        in_specs=[tile, _const_spec((1, D)), _const_spec((D, 3 * D)), _const_spec((1, LANES)),
                  _const_spec((1, LANES))],
        out_specs=[tile] * 3,
        out_shape=[jax.ShapeDtypeStruct((B, S, D), BF16)] * 3,
        compiler_params=_params(),
        name="l1_front",
    )(x, g, w_qkv, qg2, kg2)


def _na_kernel(q_ref, k_ref, v_ref, bias_ref, o_ref, *, rows_per_step, n_rows):
    i = pl.program_id(2)
    win = NA_KH * GRID_W
    lo_mask = lax.broadcasted_iota(jnp.int32, (1, LANES), 1) < NA_HEAD_DIM
    zero = jnp.zeros((), BF16)
    for rr in range(rows_per_step):
        r = i * rows_per_step + rr
        r0 = jnp.clip(r - NA_KH // 2, 0, n_rows - NA_KH)
        delta = r - r0
        start = pl.multiple_of(r0 * GRID_W, GRID_W)
        q_row = q_ref[0, rr * GRID_W:(rr + 1) * GRID_W, :]
        q2 = jnp.concatenate([jnp.where(lo_mask, q_row, zero), jnp.where(lo_mask, zero, q_row)], axis=0)
        k_win = k_ref[0, pl.ds(start, win), :]
        v_win = v_ref[0, pl.ds(start, win), :]
        bias = jnp.concatenate([bias_ref[0, delta], bias_ref[1, delta]], axis=0)
        s = _dot_nt(q2, k_win) + bias
        e = jnp.exp(s - jnp.max(s, axis=-1, keepdims=True))
        denom = jnp.sum(e, axis=-1, keepdims=True)
        o2 = _dot(e.astype(BF16), v_win) * (1.0 / denom)
        out = jnp.where(lo_mask, o2[:GRID_W], o2[GRID_W:])
        o_ref[0, rr * GRID_W:(rr + 1) * GRID_W, :] = out.astype(BF16)


def _na_attn(q, k, v, bias):
    B, S, D = q.shape
    n_rows = S // GRID_W
    tq = NA_ROWS_PER_STEP * GRID_W
    kv_spec = pl.BlockSpec((1, S, LANES), lambda b, j, i: (b, 0, j))
    return pl.pallas_call(
        functools.partial(_na_kernel, rows_per_step=NA_ROWS_PER_STEP, n_rows=n_rows),
        grid=(B, D // LANES, S // tq),
        in_specs=[
            pl.BlockSpec((1, tq, LANES), lambda b, j, i: (b, i, j)),
            kv_spec, kv_spec,
            pl.BlockSpec((2, NA_KH, GRID_W, NA_KH * GRID_W), lambda b, j, i: (j, 0, 0, 0)),
        ],
        out_specs=pl.BlockSpec((1, tq, LANES), lambda b, j, i: (b, i, j)),
        out_shape=jax.ShapeDtypeStruct((B, S, D), BF16),
        compiler_params=_params(),
        name="na_attn",
    )(q, k, v, bias)


def _na_bias_table(rpb):
    cols = jnp.arange(GRID_W)
    c0 = jnp.clip(cols - NA_KW // 2, 0, GRID_W - NA_KW)
    kc = jnp.arange(GRID_W)
    valid = (kc[None, :] >= c0[:, None]) & (kc[None, :] < c0[:, None] + NA_KW)
    dc = jnp.clip(kc[None, :] - cols[:, None] + (NA_KW - 1), 0, 2 * NA_KW - 2)
    t = jnp.where(valid[None, None], rpb[:, :, dc], MASK_VALUE)
    per_delta = []
    for delta in range(NA_KH):
        rows = t[:, NA_KH - 1 - delta:2 * NA_KH - 1 - delta]
        per_delta.append(rows.transpose(0, 2, 1, 3).reshape(rpb.shape[0], GRID_W, NA_KH * GRID_W))
    return jnp.stack(per_delta, axis=1)


def _pad_heads(w, heads, width):
    lead = w.shape[:-1]
    w = w.reshape(*lead, heads, width)
    w = jnp.pad(w, [(0, 0)] * len(lead) + [(0, 0), (0, LANES - width)])
    return w.reshape(*lead, heads * LANES)


def _rope_tables(seq_len):
    half = MLA_ROPE // 2
    pos = jnp.arange(seq_len, dtype=F32)
    freqs = ROPE_THETA ** (-jnp.arange(half, dtype=F32) / half)
    ang = pos[:, None] * freqs[None, :]
    cos, sin = jnp.cos(ang), jnp.sin(ang)
    z = lambda n: jnp.zeros((seq_len, n), F32)
    cos_t = jnp.concatenate([jnp.ones((seq_len, MLA_NOPE), F32), cos, cos, z(LANES - MLA_QK)], axis=1)
    sin_a = jnp.concatenate([z(MLA_NOPE), -sin, z(half), z(LANES - MLA_QK)], axis=1)
    sin_b = jnp.concatenate([z(MLA_NOPE), z(half), sin, z(LANES - MLA_QK)], axis=1)
    return cos_t, sin_a, sin_b


def kernel(x, mem, mix_norm_g, xattn_norm_g, ff_norm_g, w_mem_q, mem_q_g, w_mem_o, w_ff1, w_ff2, mem_tok_norm_g, w_mem_kv, mem_k_g, w_in_e, pool_w, pool_scale, q_lora_g, w_uq, kv_lora_g, w_ukv, mla_q_g, mla_k_g, w_out_e, w_qkv_o, na_q_g, na_k_g, na_rpb, w_out_o):
    B, S, D = x.shape
    row = lambda v: v.reshape(1, -1)

    mem_k, mem_v = _mem_kv(mem, row(mem_tok_norm_g), w_mem_kv.astype(BF16), row(mem_k_g))

    o1 = POOL_WIDTH
    o3 = o1 + Q_LORA + KV_LORA
    w_in = w_in_e[0]
    w_in_pad = jnp.concatenate(
        [w_in[:, :o3], jnp.zeros((D, MLA_NOPE), F32), w_in[:, o3:], jnp.zeros((D, LANES - MLA_QK), F32)],
        axis=1).astype(BF16)
    wuq = _pad_heads(w_uq[0], MLA_HEADS, MLA_QK).astype(BF16)
    w_ukv_h = w_ukv[0].reshape(KV_LORA, MLA_HEADS, MLA_NOPE + MLA_V)
    wuk = _pad_heads(w_ukv_h[:, :, :MLA_NOPE].reshape(KV_LORA, -1), MLA_HEADS, MLA_NOPE).astype(BF16)
    wuv = _pad_heads(w_ukv_h[:, :, MLA_NOPE:].reshape(KV_LORA, -1), MLA_HEADS, MLA_V).astype(BF16)
    qg = jnp.pad(mla_q_g[0], (0, LANES - MLA_QK)).reshape(1, LANES)
    kg = jnp.pad(mla_k_g[0], (0, LANES - MLA_QK)).reshape(1, LANES)
    cos_t, sin_a, sin_b = _rope_tables(S)
    up, q, k, v = _l0_front(x, row(mix_norm_g[0]), w_in_pad, row(q_lora_g[0]), wuq, row(kv_lora_g[0]),
                            wuk, wuv, qg, kg, cos_t, sin_a, sin_b)
    o = _mla_attn(q, k, v)
    w_out = w_out_e[0]
    wa = w_out[:POOL_WIDTH].astype(BF16)
    wb = jnp.pad(w_out[POOL_WIDTH:].reshape(MLA_HEADS, MLA_V, D), ((0, 0), (0, LANES - MLA_V), (0, 0)))
    wb = wb.reshape(MLA_HEADS * LANES, D).astype(BF16)
    x = _l0_out(x, up, pool_w[0].astype(BF16), row(pool_scale[0]), wa, o, wb,
                row(xattn_norm_g[0]), w_mem_q[0].astype(BF16), row(mem_q_g[0]), mem_k, mem_v,
                w_mem_o[0].astype(BF16))
    x = _mlp(x, row(ff_norm_g[0]), w_ff1[0].astype(BF16), w_ff2[0].astype(BF16))

    qg2 = jnp.tile(na_q_g[0], LANES // NA_HEAD_DIM).reshape(1, LANES)
    kg2 = jnp.tile(na_k_g[0], LANES // NA_HEAD_DIM).reshape(1, LANES)
    q, k, v = _l1_front(x, row(mix_norm_g[1]), w_qkv_o[0].astype(BF16), qg2, kg2)
    c = _na_attn(q, k, v, _na_bias_table(na_rpb[0]))
    x = _l1_out(x, c, w_out_o[0].astype(BF16), row(xattn_norm_g[1]), w_mem_q[1].astype(BF16),
                row(mem_q_g[1]), mem_k, mem_v, w_mem_o[1].astype(BF16))
    x = _mlp(x, row(ff_norm_g[1]), w_ff1[1].astype(BF16), w_ff2[1].astype(BF16))
    return x
```

```python
import functools

import jax
import jax.numpy as jnp
from jax import lax
from jax.experimental import pallas as pl
from jax.experimental.pallas import tpu as pltpu

F32 = jnp.float32
BF16 = jnp.bfloat16

D_MODEL = 1024
GRID_W = 64
EPS = 1e-6
POOL_WIDTH = 512
POOL_GROUP_W = 128
POOL_WINDOWS = (2, 4, 8, 16)
POOL_HALO = 8
MLA_HEADS = 8
MLA_NOPE = 64
MLA_ROPE = 32
MLA_V = 64
MLA_QK = MLA_NOPE + MLA_ROPE
Q_LORA = 256
KV_LORA = 128
ROPE_THETA = 10000.0
NA_HEADS = 16
NA_HEAD_DIM = 64
NA_KH = 8
NA_KW = 16
MEM_HEADS = 4
MEM_HEAD_DIM = 256
D_FF = 4096

LANES = 128
SUBLANES = 8
MASK_VALUE = -1e30
LOG2_E = 1.4426950408889634
VMEM_LIMIT = 48 * 1024 * 1024

TOKEN_TILE = 512
MLA_Q_TILE = 512
MLA_KV_TILE = 512
NA_ROWS_PER_STEP = 8


def _const_spec(shape):
    zeros = (0,) * len(shape)
    return pl.BlockSpec(shape, lambda *_: zeros, pipeline_mode=pl.Buffered(1))


def _params(**flags):
    return pltpu.CompilerParams(vmem_limit_bytes=VMEM_LIMIT, flags=flags or None)


def _rms(x, g, n=None):
    n = x.shape[-1] if n is None else n
    ms = jnp.sum(x * x, axis=-1, keepdims=True) * (1.0 / n)
    return x * lax.rsqrt(ms + EPS) * g


def _dot(a, b):
    return jnp.dot(a, b, preferred_element_type=F32)


def _dot_nt(a, b):
    return lax.dot_general(a, b, (((1,), (1,)), ((), ())), preferred_element_type=F32)


def _mem_kv_kernel(mem_ref, g_ref, w_ref, kg_ref, k_out, v_out):
    mn = _rms(mem_ref[0], g_ref[...]).astype(BF16)
    kv = _dot(mn, w_ref[...])
    for h in range(MEM_HEADS):
        sl = slice(h * MEM_HEAD_DIM, (h + 1) * MEM_HEAD_DIM)
        k_out[0, :, sl] = _rms(kv[:, sl], kg_ref[...]).astype(BF16)
    v_out[0] = kv[:, D_MODEL:].astype(BF16)


def _mem_kv(mem, g, w_kv, k_g):
    B, N, D = mem.shape
    return pl.pallas_call(
        _mem_kv_kernel,
        grid=(B,),
        in_specs=[
            pl.BlockSpec((1, N, D), lambda b: (b, 0, 0)),
            _const_spec((1, D)),
            _const_spec((D, 2 * D)),
            _const_spec((1, MEM_HEAD_DIM)),
        ],
        out_specs=[pl.BlockSpec((1, N, D), lambda b: (b, 0, 0))] * 2,
        out_shape=[jax.ShapeDtypeStruct((B, N, D), BF16)] * 2,
        compiler_params=_params(),
        name="mem_kv",
    )(mem, g, w_kv, k_g)


def _rope(x, cos, sin_a, sin_b):
    return x * cos + pltpu.roll(x, LANES - 16, 1) * sin_a + pltpu.roll(x, 16, 1) * sin_b


def _l0_front_kernel(x_ref, g_ref, win_ref, qlg_ref, wuq_ref, kvlg_ref, wuk_ref, wuv_ref,
                     qg_ref, kg_ref, cos_ref, sa_ref, sb_ref,
                     up_ref, q_ref, k_ref, v_ref):
    h = _rms(x_ref[0], g_ref[...]).astype(BF16)
    u = _dot(h, win_ref[...])
    up_ref[0] = u[:, :POOL_WIDTH]
    o2 = POOL_WIDTH + Q_LORA
    o3 = o2 + KV_LORA
    cq = _rms(u[:, POOL_WIDTH:o2], qlg_ref[...]).astype(BF16)
    ckv = _rms(u[:, o2:o3], kvlg_ref[...]).astype(BF16)
    kr = u[:, o3:]
    qf = _dot(cq, wuq_ref[...])
    kf = _dot(ckv, wuk_ref[...])
    vf = _dot(ckv, wuv_ref[...])
    cos, sa, sb = cos_ref[...], sa_ref[...], sb_ref[...]
    lane = lax.broadcasted_iota(jnp.int32, (1, LANES), 1)
    scale = MLA_QK ** -0.5 * LOG2_E
    for hd in range(MLA_HEADS):
        sl = slice(hd * LANES, (hd + 1) * LANES)
        qh = _rope(_rms(qf[:, sl], qg_ref[...], n=MLA_QK), cos, sa, sb) * scale
        q_ref[0, hd] = qh.T.astype(BF16)
        kh = _rope(_rms(kf[:, sl] + kr, kg_ref[...], n=MLA_QK), cos, sa, sb)
        k_ref[0, hd] = kh.astype(BF16)
        v_ref[0, hd] = jnp.where(lane == MLA_V, 1.0, vf[:, sl]).T.astype(BF16)


def _l0_front(x, g, w_in, qlg, wuq, kvlg, wuk, wuv, qg, kg, cos, sa, sb):
    B, S, D = x.shape
    tm = TOKEN_TILE
    hw = MLA_HEADS * LANES
    head_spec = pl.BlockSpec((1, MLA_HEADS, tm, LANES), lambda b, i: (b, 0, i, 0))
    head_t_spec = pl.BlockSpec((1, MLA_HEADS, LANES, tm), lambda b, i: (b, 0, 0, i))
    tab_spec = pl.BlockSpec((tm, LANES), lambda b, i: (i, 0))
    return pl.pallas_call(
        _l0_front_kernel,
        grid=(B, S // tm),
        in_specs=[
            pl.BlockSpec((1, tm, D), lambda b, i: (b, i, 0)),
            _const_spec((1, D)),
            _const_spec((D, D)),
            _const_spec((1, Q_LORA)),
            _const_spec((Q_LORA, hw)),
            _const_spec((1, KV_LORA)),
            _const_spec((KV_LORA, hw)),
            _const_spec((KV_LORA, hw)),
            _const_spec((1, LANES)),
            _const_spec((1, LANES)),
            tab_spec, tab_spec, tab_spec,
        ],
        out_specs=[
            pl.BlockSpec((1, tm, POOL_WIDTH), lambda b, i: (b, i, 0)),
            head_t_spec, head_spec, head_t_spec,
        ],
        out_shape=[
            jax.ShapeDtypeStruct((B, S, POOL_WIDTH), F32),
            jax.ShapeDtypeStruct((B, MLA_HEADS, LANES, S), BF16),
            jax.ShapeDtypeStruct((B, MLA_HEADS, S, LANES), BF16),
            jax.ShapeDtypeStruct((B, MLA_HEADS, LANES, S), BF16),
        ],
        compiler_params=_params(),
        name="l0_front",
    )(x, g, w_in, qlg, wuq, kvlg, wuk, wuv, qg, kg, cos, sa, sb)


def _mla_attn_kernel(qt_ref, k_ref, vt_ref, o_ref, s0_ref, s1_ref, s2_ref, acc_ref, *, tk):
    qt = qt_ref[0, 0]
    nk = k_ref.shape[2] // tk
    bufs = (s0_ref, s1_ref, s2_ref)
    z_st = pl.multiple_of(jnp.minimum(pl.program_id(2), 0) * SUBLANES, SUBLANES)
    z_ld = pl.multiple_of(jnp.minimum(pl.program_id(1), 0) * SUBLANES, SUBLANES)
    st_rows, ld_rows = pl.ds(z_st, tk), pl.ds(z_ld, tk)
    acc_st, acc_rows = pl.ds(z_st, LANES), pl.ds(z_ld, LANES)

    def fill(c):
        bufs[c % 3][st_rows, :] = _dot(k_ref[0, 0, c * tk:(c + 1) * tk, :], qt)

    def update(c, m_old):
        s_ref = bufs[c % 3]
        vt = vt_ref[0, 0, :, c * tk:(c + 1) * tk]
        cm = jnp.max(s_ref[ld_rows, :], axis=0, keepdims=True)
        m_new = cm if c == 0 else jnp.maximum(m_old, cm)
        p = jnp.exp2((s_ref[ld_rows, :] - m_new).astype(BF16))
        pv = _dot(vt, p)
        if c == 0:
            acc_ref[acc_st, :] = pv
        else:
            acc_ref[acc_st, :] = acc_ref[acc_rows, :] * jnp.exp2(m_old - m_new) + pv
        return m_new

    fill(0)
    fill(1)
    m = None
    for c in range(nk):
        if c + 2 < nk:
            fill(c + 2)
        m = update(c, m)
    acc = acc_ref[acc_rows, :]
    denom = acc[MLA_V:MLA_V + 1, :]
    o_ref[0] = (acc * (1.0 / denom)).T.astype(BF16)


def _mla_attn(qt, k, vt):
    B, H, S, L = k.shape
    tq, tk = MLA_Q_TILE, MLA_KV_TILE
    assert S % tk == 0 and S % tq == 0
    return pl.pallas_call(
        functools.partial(_mla_attn_kernel, tk=tk),
        grid=(B, H, S // tq),
        in_specs=[
            pl.BlockSpec((1, 1, L, tq), lambda b, h, i: (b, h, 0, i)),
            pl.BlockSpec((1, 1, S, L), lambda b, h, i: (b, h, 0, 0)),
            pl.BlockSpec((1, 1, L, S), lambda b, h, i: (b, h, 0, 0)),
        ],
        out_specs=pl.BlockSpec((1, tq, L), lambda b, h, i: (b, i, h)),
        out_shape=jax.ShapeDtypeStruct((B, S, H * L), BF16),
        scratch_shapes=[
            pltpu.VMEM((tk + SUBLANES, tq), F32),
            pltpu.VMEM((tk + SUBLANES, tq), F32),
            pltpu.VMEM((tk + SUBLANES, tq), F32),
            pltpu.VMEM((L + SUBLANES, tq), F32),
        ],
        compiler_params=_params(),
        name="mla_attn",
    )(qt, k, vt)


def _pool_mix(up_ref, upp_ref, upn_ref, pw_ref, ps_ref, ext_ref, tm, seq_len):
    i = pl.program_id(1)
    last = pl.num_programs(1) - 1
    ext_ref[0:POOL_HALO] = jnp.where(i > 0, upp_ref[0], 0.0)
    ext_ref[POOL_HALO:POOL_HALO + tm] = up_ref[0]
    ext_ref[POOL_HALO + tm:2 * POOL_HALO + tm] = jnp.where(i < last, upn_ref[0], 0.0)
    t = i * tm + lax.broadcasted_iota(jnp.int32, (tm, 1), 0)
    outs = []
    for g, w in enumerate(POOL_WINDOWS):
        cols = slice(g * POOL_GROUP_W, (g + 1) * POOL_GROUP_W)
        wsum = None
        for o in range(-(w // 2), w - w // 2):
            piece = ext_ref[POOL_HALO + o:POOL_HALO + o + tm, cols]
            wsum = piece if wsum is None else wsum + piece
        lo = jnp.maximum(t - w // 2, 0)
        hi = jnp.minimum(t + (w - 1 - w // 2), seq_len - 1)
        cnt = (hi - lo + 1).astype(F32)
        d = wsum / cnt - ext_ref[POOL_HALO:POOL_HALO + tm, cols]
        outs.append(_dot(d.astype(BF16), pw_ref[g]))
    return jnp.concatenate(outs, axis=-1) * ps_ref[...]


def _mem_xattn(x1, xg_ref, wq_ref, qg_ref, mk_ref, mv_ref, wo_ref):
    h2 = _rms(x1, xg_ref[...]).astype(BF16)
    qm = _dot(h2, wq_ref[...])
    scale = MEM_HEAD_DIM ** -0.5
    outs = []
    for hd in range(MEM_HEADS):
        sl = slice(hd * MEM_HEAD_DIM, (hd + 1) * MEM_HEAD_DIM)
        qh = (_rms(qm[:, sl], qg_ref[...]) * scale).astype(BF16)
        s = _dot_nt(qh, mk_ref[0, :, sl])
        e = jnp.exp(s - jnp.max(s, axis=-1, keepdims=True))
        denom = jnp.sum(e, axis=-1, keepdims=True)
        oh = _dot(e.astype(BF16), mv_ref[0, :, sl]) * (1.0 / denom)
        outs.append(oh.astype(BF16))
    return x1 + _dot(jnp.concatenate(outs, axis=-1), wo_ref[...])


def _l0_out_kernel(x_ref, up_ref, upp_ref, upn_ref, pw_ref, ps_ref, wa_ref, o_ref, wb_ref,
                   xg_ref, wq_ref, qg_ref, mk_ref, mv_ref, wo_ref, out_ref, ext_ref, *, tm, seq_len):
    a = _pool_mix(up_ref, upp_ref, upn_ref, pw_ref, ps_ref, ext_ref, tm, seq_len)
    x1 = x_ref[0] + _dot(a.astype(BF16), wa_ref[...]) + _dot(o_ref[0], wb_ref[...])
    out_ref[0] = _mem_xattn(x1, xg_ref, wq_ref, qg_ref, mk_ref, mv_ref, wo_ref)


def _l1_out_kernel(x_ref, o_ref, wb_ref, xg_ref, wq_ref, qg_ref, mk_ref, mv_ref, wo_ref, out_ref):
    x1 = x_ref[0] + _dot(o_ref[0], wb_ref[...])
    out_ref[0] = _mem_xattn(x1, xg_ref, wq_ref, qg_ref, mk_ref, mv_ref, wo_ref)


def _xattn_specs(n_mem):
    D = D_MODEL
    return [
        _const_spec((1, D)),
        _const_spec((D, D)),
        _const_spec((1, MEM_HEAD_DIM)),
        pl.BlockSpec((1, n_mem, D), lambda b, i: (b, 0, 0)),
        pl.BlockSpec((1, n_mem, D), lambda b, i: (b, 0, 0)),
        _const_spec((D, D)),
    ]


def _l0_out(x, up, pool_w, pool_scale, wa, o, wb, xg, wq, qg, mk, mv, wo):
    B, S, D = x.shape
    tm = TOKEN_TILE
    halo_blocks = tm // POOL_HALO
    tile = pl.BlockSpec((1, tm, D), lambda b, i: (b, i, 0))
    return pl.pallas_call(
        functools.partial(_l0_out_kernel, tm=tm, seq_len=S),
        grid=(B, S // tm),
        in_specs=[
            tile,
            pl.BlockSpec((1, tm, POOL_WIDTH), lambda b, i: (b, i, 0)),
            pl.BlockSpec((1, POOL_HALO, POOL_WIDTH),
                         lambda b, i: (b, jnp.maximum(i * halo_blocks - 1, 0), 0)),
            pl.BlockSpec((1, POOL_HALO, POOL_WIDTH),
                         lambda b, i: (b, jnp.minimum((i + 1) * halo_blocks, S // POOL_HALO - 1), 0)),
            _const_spec(pool_w.shape),
            _const_spec((1, POOL_WIDTH)),
            _const_spec((POOL_WIDTH, D)),
            pl.BlockSpec((1, tm, o.shape[-1]), lambda b, i: (b, i, 0)),
            _const_spec(wb.shape),
        ] + _xattn_specs(mk.shape[1]),
        out_specs=tile,
        out_shape=jax.ShapeDtypeStruct((B, S, D), F32),
        scratch_shapes=[pltpu.VMEM((tm + 2 * POOL_HALO, POOL_WIDTH), F32)],
        compiler_params=_params(),
        name="l0_out_xattn",
    )(x, up, up, up, pool_w, pool_scale, wa, o, wb, xg, wq, qg, mk, mv, wo)


def _l1_out(x, o, wb, xg, wq, qg, mk, mv, wo):
    B, S, D = x.shape
    tm = TOKEN_TILE
    tile = pl.BlockSpec((1, tm, D), lambda b, i: (b, i, 0))
    return pl.pallas_call(
        _l1_out_kernel,
        grid=(B, S // tm),
        in_specs=[tile, pl.BlockSpec((1, tm, o.shape[-1]), lambda b, i: (b, i, 0)), _const_spec(wb.shape)]
        + _xattn_specs(mk.shape[1]),
        out_specs=tile,
        out_shape=jax.ShapeDtypeStruct((B, S, D), F32),
        compiler_params=_params(),
        name="l1_out_xattn",
    )(x, o, wb, xg, wq, qg, mk, mv, wo)


def _mlp_kernel(x_ref, g_ref, w1_ref, w2_ref, out_ref, *, ff_chunk):
    x = x_ref[0]
    h = _rms(x, g_ref[...]).astype(BF16)
    acc = x
    for c in range(D_FF // ff_chunk):
        sl = slice(c * ff_chunk, (c + 1) * ff_chunk)
        a = jnp.maximum(_dot(h, w1_ref[:, sl]), 0.0)
        acc = acc + _dot((a * a).astype(BF16), w2_ref[sl, :])
    out_ref[0] = acc


def _mlp(x, g, w1, w2):
    B, S, D = x.shape
    tm = TOKEN_TILE
    tile = pl.BlockSpec((1, tm, D), lambda b, i: (b, i, 0))
    return pl.pallas_call(
        functools.partial(_mlp_kernel, ff_chunk=1024),
        grid=(B, S // tm),
        in_specs=[tile, _const_spec((1, D)), _const_spec((D, D_FF)), _const_spec((D_FF, D))],
        out_specs=tile,
        out_shape=jax.ShapeDtypeStruct((B, S, D), F32),
        compiler_params=_params(),
        name="mlp",
    )(x, g, w1, w2)


def _pair_norm(z, g2, lo_mask):
    y = z * z
    lo = jnp.sum(jnp.where(lo_mask, y, 0.0), axis=-1, keepdims=True)
    hi = jnp.sum(jnp.where(lo_mask, 0.0, y), axis=-1, keepdims=True)
    inv = 1.0 / NA_HEAD_DIM
    r = jnp.where(lo_mask, lax.rsqrt(lo * inv + EPS), lax.rsqrt(hi * inv + EPS))
    return z * r * g2


def _l1_front_kernel(x_ref, g_ref, w_ref, qg_ref, kg_ref, q_out, k_out, v_out):
    h = _rms(x_ref[0], g_ref[...]).astype(BF16)
    qkv = _dot(h, w_ref[...])
    width = NA_HEADS * NA_HEAD_DIM
    lo_mask = lax.broadcasted_iota(jnp.int32, (1, LANES), 1) < NA_HEAD_DIM
    scale = NA_HEAD_DIM ** -0.5
    for j in range(width // LANES):
        sl = slice(j * LANES, (j + 1) * LANES)
        q_out[0, :, sl] = (_pair_norm(qkv[:, sl], qg_ref[...], lo_mask) * scale).astype(BF16)
        ksl = slice(width + j * LANES, width + (j + 1) * LANES)
        k_out[0, :, sl] = _pair_norm(qkv[:, ksl], kg_ref[...], lo_mask).astype(BF16)
    v_out[0] = qkv[:, 2 * width:].astype(BF16)


def _l1_front(x, g, w_qkv, qg2, kg2):
    B, S, D = x.shape
    tm = TOKEN_TILE
    tile = pl.BlockSpec((1, tm, D), lambda b, i: (b, i, 0))
    return pl.pallas_call(
        _l1_front_kernel,
        grid=(B, S // tm),
        in_specs=[tile, _const_spec((1, D)), _const_spec((D, 3 * D)), _const_spec((1, LANES)),
                  _const_spec((1, LANES))],
        out_specs=[tile] * 3,
        out_shape=[jax.ShapeDtypeStruct((B, S, D), BF16)] * 3,
        compiler_params=_params(),
        name="l1_front",
    )(x, g, w_qkv, qg2, kg2)


def _na_kernel(q_ref, k_ref, v_ref, bias_ref, o_ref, *, rows_per_step, n_rows):
    i = pl.program_id(2)
    win = NA_KH * GRID_W
    lo_mask = lax.broadcasted_iota(jnp.int32, (1, LANES), 1) < NA_HEAD_DIM
    zero = jnp.zeros((), BF16)
    for rr in range(rows_per_step):
        r = i * rows_per_step + rr
        r0 = jnp.clip(r - NA_KH // 2, 0, n_rows - NA_KH)
        delta = r - r0
        start = pl.multiple_of(r0 * GRID_W, GRID_W)
        q_row = q_ref[0, rr * GRID_W:(rr + 1) * GRID_W, :]
        q2 = jnp.concatenate([jnp.where(lo_mask, q_row, zero), jnp.where(lo_mask, zero, q_row)], axis=0)
        k_win = k_ref[0, pl.ds(start, win), :]
        v_win = v_ref[0, pl.ds(start, win), :]
        bias = jnp.concatenate([bias_ref[0, delta], bias_ref[1, delta]], axis=0)
        s = _dot_nt(q2, k_win) + bias
        e = jnp.exp(s - jnp.max(s, axis=-1, keepdims=True))
        denom = jnp.sum(e, axis=-1, keepdims=True)
        o2 = _dot(e.astype(BF16), v_win) * (1.0 / denom)
        out = jnp.where(lo_mask, o2[:GRID_W], o2[GRID_W:])
        o_ref[0, rr * GRID_W:(rr + 1) * GRID_W, :] = out.astype(BF16)


def _na_attn(q, k, v, bias):
    B, S, D = q.shape
    n_rows = S // GRID_W
    tq = NA_ROWS_PER_STEP * GRID_W
    kv_spec = pl.BlockSpec((1, S, LANES), lambda b, j, i: (b, 0, j))
    return pl.pallas_call(
        functools.partial(_na_kernel, rows_per_step=NA_ROWS_PER_STEP, n_rows=n_rows),
        grid=(B, D // LANES, S // tq),
        in_specs=[
            pl.BlockSpec((1, tq, LANES), lambda b, j, i: (b, i, j)),
            kv_spec, kv_spec,
            pl.BlockSpec((2, NA_KH, GRID_W, NA_KH * GRID_W), lambda b, j, i: (j, 0, 0, 0)),
        ],
        out_specs=pl.BlockSpec((1, tq, LANES), lambda b, j, i: (b, i, j)),
        out_shape=jax.ShapeDtypeStruct((B, S, D), BF16),
        compiler_params=_params(),
        name="na_attn",
    )(q, k, v, bias)


def _na_bias_table(rpb):
    cols = jnp.arange(GRID_W)
    c0 = jnp.clip(cols - NA_KW // 2, 0, GRID_W - NA_KW)
    kc = jnp.arange(GRID_W)
    valid = (kc[None, :] >= c0[:, None]) & (kc[None, :] < c0[:, None] + NA_KW)
    dc = jnp.clip(kc[None, :] - cols[:, None] + (NA_KW - 1), 0, 2 * NA_KW - 2)
    t = jnp.where(valid[None, None], rpb[:, :, dc], MASK_VALUE)
    per_delta = []
    for delta in range(NA_KH):
        rows = t[:, NA_KH - 1 - delta:2 * NA_KH - 1 - delta]
        per_delta.append(rows.transpose(0, 2, 1, 3).reshape(rpb.shape[0], GRID_W, NA_KH * GRID_W))
    return jnp.stack(per_delta, axis=1)


def _pad_heads(w, heads, width):
    lead = w.shape[:-1]
    w = w.reshape(*lead, heads, width)
    w = jnp.pad(w, [(0, 0)] * len(lead) + [(0, 0), (0, LANES - width)])
    return w.reshape(*lead, heads * LANES)


def _rope_tables(seq_len):
    half = MLA_ROPE // 2
    pos = jnp.arange(seq_len, dtype=F32)
    freqs = ROPE_THETA ** (-jnp.arange(half, dtype=F32) / half)
    ang = pos[:, None] * freqs[None, :]
    cos, sin = jnp.cos(ang), jnp.sin(ang)
    z = lambda n: jnp.zeros((seq_len, n), F32)
    cos_t = jnp.concatenate([jnp.ones((seq_len, MLA_NOPE), F32), cos, cos, z(LANES - MLA_QK)], axis=1)
    sin_a = jnp.concatenate([z(MLA_NOPE), -sin, z(half), z(LANES - MLA_QK)], axis=1)
    sin_b = jnp.concatenate([z(MLA_NOPE), z(half), sin, z(LANES - MLA_QK)], axis=1)
    return cos_t, sin_a, sin_b


def kernel(x, mem, mix_norm_g, xattn_norm_g, ff_norm_g, w_mem_q, mem_q_g, w_mem_o, w_ff1, w_ff2, mem_tok_norm_g, w_mem_kv, mem_k_g, w_in_e, pool_w, pool_scale, q_lora_g, w_uq, kv_lora_g, w_ukv, mla_q_g, mla_k_g, w_out_e, w_qkv_o, na_q_g, na_k_g, na_rpb, w_out_o):
    B, S, D = x.shape
    row = lambda v: v.reshape(1, -1)

    mem_k, mem_v = _mem_kv(mem, row(mem_tok_norm_g), w_mem_kv.astype(BF16), row(mem_k_g))

    o1 = POOL_WIDTH
    o3 = o1 + Q_LORA + KV_LORA
    w_in = w_in_e[0]
    w_in_pad = jnp.concatenate(
        [w_in[:, :o3], jnp.zeros((D, MLA_NOPE), F32), w_in[:, o3:], jnp.zeros((D, LANES - MLA_QK), F32)],
        axis=1).astype(BF16)
    wuq = _pad_heads(w_uq[0], MLA_HEADS, MLA_QK).astype(BF16)
    w_ukv_h = w_ukv[0].reshape(KV_LORA, MLA_HEADS, MLA_NOPE + MLA_V)
    wuk = _pad_heads(w_ukv_h[:, :, :MLA_NOPE].reshape(KV_LORA, -1), MLA_HEADS, MLA_NOPE).astype(BF16)
    wuv = _pad_heads(w_ukv_h[:, :, MLA_NOPE:].reshape(KV_LORA, -1), MLA_HEADS, MLA_V).astype(BF16)
    qg = jnp.pad(mla_q_g[0], (0, LANES - MLA_QK)).reshape(1, LANES)
    kg = jnp.pad(mla_k_g[0], (0, LANES - MLA_QK)).reshape(1, LANES)
    cos_t, sin_a, sin_b = _rope_tables(S)
    up, q, k, v = _l0_front(x, row(mix_norm_g[0]), w_in_pad, row(q_lora_g[0]), wuq, row(kv_lora_g[0]),
                            wuk, wuv, qg, kg, cos_t, sin_a, sin_b)
    o = _mla_attn(q, k, v)
    w_out = w_out_e[0]
    wa = w_out[:POOL_WIDTH].astype(BF16)
    wb = jnp.pad(w_out[POOL_WIDTH:].reshape(MLA_HEADS, MLA_V, D), ((0, 0), (0, LANES - MLA_V), (0, 0)))
    wb = wb.reshape(MLA_HEADS * LANES, D).astype(BF16)
    x = _l0_out(x, up, pool_w[0].astype(BF16), row(pool_scale[0]), wa, o, wb,
                row(xattn_norm_g[0]), w_mem_q[0].astype(BF16), row(mem_q_g[0]), mem_k, mem_v,
                w_mem_o[0].astype(BF16))
    x = _mlp(x, row(ff_norm_g[0]), w_ff1[0].astype(BF16), w_ff2[0].astype(BF16))

    qg2 = jnp.tile(na_q_g[0], LANES // NA_HEAD_DIM).reshape(1, LANES)
    kg2 = jnp.tile(na_k_g[0], LANES // NA_HEAD_DIM).reshape(1, LANES)
    q, k, v = _l1_front(x, row(mix_norm_g[1]), w_qkv_o[0].astype(BF16), qg2, kg2)
    c = _na_attn(q, k, v, _na_bias_table(na_rpb[0]))
    x = _l1_out(x, c, w_out_o[0].astype(BF16), row(xattn_norm_g[1]), w_mem_q[1].astype(BF16),
                row(mem_q_g[1]), mem_k, mem_v, w_mem_o[1].astype(BF16))
    x = _mlp(x, row(ff_norm_g[1]), w_ff1[1].astype(BF16), w_ff2[1].astype(BF16))
    return x
```

```python
import functools

import jax
import jax.numpy as jnp
from jax import lax
from jax.experimental import pallas as pl
from jax.experimental.pallas import tpu as pltpu

F32 = jnp.float32
BF16 = jnp.bfloat16

D_MODEL = 1024
GRID_W = 64
EPS = 1e-6
POOL_WIDTH = 512
POOL_GROUP_W = 128
POOL_WINDOWS = (2, 4, 8, 16)
POOL_HALO = 8
MLA_HEADS = 8
MLA_NOPE = 64
MLA_ROPE = 32
MLA_V = 64
MLA_QK = MLA_NOPE + MLA_ROPE
Q_LORA = 256
KV_LORA = 128
ROPE_THETA = 10000.0
NA_HEADS = 16
NA_HEAD_DIM = 64
NA_KH = 8
NA_KW = 16
MEM_HEADS = 4
MEM_HEAD_DIM = 256
D_FF = 4096

LANES = 128
SUBLANES = 8
MASK_VALUE = -1e30
LOG2_E = 1.4426950408889634
VMEM_LIMIT = 48 * 1024 * 1024

TOKEN_TILE = 512
MLA_Q_TILE = 512
MLA_KV_TILE = 512
NA_ROWS_PER_STEP = 16
NA_ROWS_AHEAD = 4


def _const_spec(shape):
    zeros = (0,) * len(shape)
    return pl.BlockSpec(shape, lambda *_: zeros, pipeline_mode=pl.Buffered(1))


def _params(**flags):
    return pltpu.CompilerParams(vmem_limit_bytes=VMEM_LIMIT, flags=flags or None)


def _rms(x, g, n=None):
    n = x.shape[-1] if n is None else n
    ms = jnp.sum(x * x, axis=-1, keepdims=True) * (1.0 / n)
    return x * lax.rsqrt(ms + EPS) * g


def _dot(a, b):
    return jnp.dot(a, b, preferred_element_type=F32)


def _dot_nt(a, b):
    return lax.dot_general(a, b, (((1,), (1,)), ((), ())), preferred_element_type=F32)


def _mem_kv_kernel(mem_ref, g_ref, w_ref, kg_ref, k_out, v_out):
    mn = _rms(mem_ref[0], g_ref[...]).astype(BF16)
    kv = _dot(mn, w_ref[...])
    for h in range(MEM_HEADS):
        sl = slice(h * MEM_HEAD_DIM, (h + 1) * MEM_HEAD_DIM)
        k_out[0, :, sl] = _rms(kv[:, sl], kg_ref[...]).astype(BF16)
    v_out[0] = kv[:, D_MODEL:].astype(BF16)


def _mem_kv(mem, g, w_kv, k_g):
    B, N, D = mem.shape
    return pl.pallas_call(
        _mem_kv_kernel,
        grid=(B,),
        in_specs=[
            pl.BlockSpec((1, N, D), lambda b: (b, 0, 0)),
            _const_spec((1, D)),
            _const_spec((D, 2 * D)),
            _const_spec((1, MEM_HEAD_DIM)),
        ],
        out_specs=[pl.BlockSpec((1, N, D), lambda b: (b, 0, 0))] * 2,
        out_shape=[jax.ShapeDtypeStruct((B, N, D), BF16)] * 2,
        compiler_params=_params(),
        name="mem_kv",
    )(mem, g, w_kv, k_g)


def _l0_front_kernel(x_ref, g_ref, win_ref, qlg_ref, wuqt_ref, kvlg_ref, wuk_ref, wuvt_ref,
                     qg_ref, kg_ref, cos_ref, sa_ref, sb_ref, cost_ref, sint_ref,
                     up_ref, q_ref, k_ref, v_ref):
    tm = x_ref.shape[1]
    h = _rms(x_ref[0], g_ref[...]).astype(BF16)
    u = _dot(h, win_ref[...])
    up_ref[0] = u[:, :POOL_WIDTH]
    o2 = POOL_WIDTH + Q_LORA
    o3 = o2 + KV_LORA
    cq = _rms(u[:, POOL_WIDTH:o2], qlg_ref[...])
    ckv = _rms(u[:, o2:o3], kvlg_ref[...])
    kr = u[:, o3:]
    qt = _dot(wuqt_ref[...], cq.T.astype(BF16))
    vt = _dot(wuvt_ref[...], ckv.T.astype(BF16))
    kf = _dot(ckv.astype(BF16), wuk_ref[...])

    half = MLA_ROPE // 2
    r1 = slice(MLA_NOPE, MLA_NOPE + half)
    r2 = slice(MLA_NOPE + half, MLA_QK)
    cos_t, sin_t = cost_ref[...], sint_ref[...]
    gq = jnp.tile(qg_ref[...], (1, tm // LANES))
    row_id = lax.broadcasted_iota(jnp.int32, (LANES, tm), 0)
    inv_n = 1.0 / MLA_QK
    for hd in range(MLA_HEADS):
        rows = slice(hd * LANES, (hd + 1) * LANES)
        blk = qt[rows]
        ms = jnp.sum(blk * blk, axis=0, keepdims=True) * inv_n
        y = blk * lax.rsqrt(ms + EPS) * gq
        y1, y2 = y[r1], y[r2]
        q_ref[0, hd] = jnp.concatenate(
            [y[:MLA_NOPE], y1 * cos_t - y2 * sin_t, y1 * sin_t + y2 * cos_t, y[MLA_QK:]], axis=0).astype(BF16)
        v_ref[0, hd] = jnp.where(row_id == MLA_V, 1.0, vt[rows]).astype(BF16)

    krg = kr * kg_ref[...]
    partner = pltpu.roll(krg, LANES - half, 1) * sa_ref[...] + pltpu.roll(krg, half, 1) * sb_ref[...]
    gcos = kg_ref[...] * cos_ref[...]
    for hd in range(MLA_HEADS):
        pre = kf[:, hd * LANES:(hd + 1) * LANES] + kr
        ms = jnp.sum(pre * pre, axis=-1, keepdims=True) * inv_n
        k_ref[0, hd] = ((pre * gcos + partner) * lax.rsqrt(ms + EPS)).astype(BF16)


def _l0_front(x, g, w_in, qlg, wuqt, kvlg, wuk, wuvt, qg, kg, cos, sa, sb, cos_t, sin_t):
    B, S, D = x.shape
    tm = TOKEN_TILE
    hw = MLA_HEADS * LANES
    head_spec = pl.BlockSpec((1, MLA_HEADS, tm, LANES), lambda b, i: (b, 0, i, 0))
    head_t_spec = pl.BlockSpec((1, MLA_HEADS, LANES, tm), lambda b, i: (b, 0, 0, i))
    tab_spec = pl.BlockSpec((tm, LANES), lambda b, i: (i, 0))
    tab_t_spec = pl.BlockSpec((MLA_ROPE // 2, tm), lambda b, i: (0, i))
    return pl.pallas_call(
        _l0_front_kernel,
        grid=(B, S // tm),
        in_specs=[
            pl.BlockSpec((1, tm, D), lambda b, i: (b, i, 0)),
            _const_spec((1, D)),
            _const_spec((D, D)),
            _const_spec((1, Q_LORA)),
            _const_spec((hw, Q_LORA)),
            _const_spec((1, KV_LORA)),
            _const_spec((KV_LORA, hw)),
            _const_spec((hw, KV_LORA)),
            _const_spec((LANES, LANES)),
            _const_spec((1, LANES)),
            tab_spec, tab_spec, tab_spec, tab_t_spec, tab_t_spec,
        ],
        out_specs=[
            pl.BlockSpec((1, tm, POOL_WIDTH), lambda b, i: (b, i, 0)),
            head_t_spec, head_spec, head_t_spec,
        ],
        out_shape=[
            jax.ShapeDtypeStruct((B, S, POOL_WIDTH), F32),
            jax.ShapeDtypeStruct((B, MLA_HEADS, LANES, S), BF16),
            jax.ShapeDtypeStruct((B, MLA_HEADS, S, LANES), BF16),
            jax.ShapeDtypeStruct((B, MLA_HEADS, LANES, S), BF16),
        ],
        compiler_params=_params(),
        name="l0_front",
    )(x, g, w_in, qlg, wuqt, kvlg, wuk, wuvt, qg, kg, cos, sa, sb, cos_t, sin_t)


def _mla_attn_kernel(qt_ref, k_ref, vt_ref, o_ref, s0_ref, s1_ref, s2_ref, acc_ref, *, tk):
    qt = qt_ref[0, 0]
    nk = k_ref.shape[2] // tk
    bufs = (s0_ref, s1_ref, s2_ref)
    z_st = pl.multiple_of(jnp.minimum(pl.program_id(2), 0) * SUBLANES, SUBLANES)
    z_ld = pl.multiple_of(jnp.minimum(pl.program_id(1), 0) * SUBLANES, SUBLANES)
    st_rows, ld_rows = pl.ds(z_st, tk), pl.ds(z_ld, tk)
    acc_st, acc_rows = pl.ds(z_st, LANES), pl.ds(z_ld, LANES)

    def fill(c):
        bufs[c % 3][st_rows, :] = _dot(k_ref[0, 0, c * tk:(c + 1) * tk, :], qt)

    def update(c, m_old):
        s_ref = bufs[c % 3]
        vt = vt_ref[0, 0, :, c * tk:(c + 1) * tk]
        cm = jnp.max(s_ref[ld_rows, :], axis=0, keepdims=True)
        m_new = cm if c == 0 else jnp.maximum(m_old, cm)
        p = jnp.exp2((s_ref[ld_rows, :] - m_new).astype(BF16))
        pv = _dot(vt, p)
        if c == 0:
            acc_ref[acc_st, :] = pv
        else:
            acc_ref[acc_st, :] = acc_ref[acc_rows, :] * jnp.exp2(m_old - m_new) + pv
        return m_new

    fill(0)
    fill(1)
    m = None
    for c in range(nk):
        if c + 2 < nk:
            fill(c + 2)
        m = update(c, m)
    acc = acc_ref[acc_rows, :]
    denom = acc[MLA_V:MLA_V + 1, :]
    o_ref[0] = (acc * (1.0 / denom)).T.astype(BF16)


def _mla_attn(qt, k, vt):
    B, H, S, L = k.shape
    tq, tk = MLA_Q_TILE, MLA_KV_TILE
    assert S % tk == 0 and S % tq == 0
    return pl.pallas_call(
        functools.partial(_mla_attn_kernel, tk=tk),
        grid=(B, H, S // tq),
        in_specs=[
            pl.BlockSpec((1, 1, L, tq), lambda b, h, i: (b, h, 0, i)),
            pl.BlockSpec((1, 1, S, L), lambda b, h, i: (b, h, 0, 0)),
            pl.BlockSpec((1, 1, L, S), lambda b, h, i: (b, h, 0, 0)),
        ],
        out_specs=pl.BlockSpec((1, tq, L), lambda b, h, i: (b, i, h)),
        out_shape=jax.ShapeDtypeStruct((B, S, H * L), BF16),
        scratch_shapes=[
            pltpu.VMEM((tk + SUBLANES, tq), F32),
            pltpu.VMEM((tk + SUBLANES, tq), F32),
            pltpu.VMEM((tk + SUBLANES, tq), F32),
            pltpu.VMEM((L + SUBLANES, tq), F32),
        ],
        compiler_params=_params(),
        name="mla_attn",
    )(qt, k, vt)


def _pool_mix(up_ref, upp_ref, upn_ref, pw_ref, ps_ref, ext_ref, tm, seq_len):
    i = pl.program_id(1)
    last = pl.num_programs(1) - 1
    ext_ref[0:POOL_HALO] = jnp.where(i > 0, upp_ref[0], 0.0)
    ext_ref[POOL_HALO:POOL_HALO + tm] = up_ref[0]
    ext_ref[POOL_HALO + tm:2 * POOL_HALO + tm] = jnp.where(i < last, upn_ref[0], 0.0)
    t = i * tm + lax.broadcasted_iota(jnp.int32, (tm, 1), 0)
    outs = []
    for g, w in enumerate(POOL_WINDOWS):
        cols = slice(g * POOL_GROUP_W, (g + 1) * POOL_GROUP_W)
        wsum = None
        for o in range(-(w // 2), w - w // 2):
            piece = ext_ref[POOL_HALO + o:POOL_HALO + o + tm, cols]
            wsum = piece if wsum is None else wsum + piece
        lo = jnp.maximum(t - w // 2, 0)
        hi = jnp.minimum(t + (w - 1 - w // 2), seq_len - 1)
        cnt = (hi - lo + 1).astype(F32)
        d = wsum / cnt - ext_ref[POOL_HALO:POOL_HALO + tm, cols]
        outs.append(_dot(d.astype(BF16), pw_ref[g]))
    return jnp.concatenate(outs, axis=-1) * ps_ref[...]


def _mem_xattn(x1, xg_ref, wq_ref, qg_ref, mk_ref, mv_ref, wo_ref):
    h2 = _rms(x1, xg_ref[...]).astype(BF16)
    qm = _dot(h2, wq_ref[...])
    scale = MEM_HEAD_DIM ** -0.5
    outs = []
    for hd in range(MEM_HEADS):
        sl = slice(hd * MEM_HEAD_DIM, (hd + 1) * MEM_HEAD_DIM)
        qh = (_rms(qm[:, sl], qg_ref[...]) * scale).astype(BF16)
        s = _dot_nt(qh, mk_ref[0, :, sl])
        e = jnp.exp(s - jnp.max(s, axis=-1, keepdims=True))
        denom = jnp.sum(e, axis=-1, keepdims=True)
        oh = _dot(e.astype(BF16), mv_ref[0, :, sl]) * (1.0 / denom)
        outs.append(oh.astype(BF16))
    return x1 + _dot(jnp.concatenate(outs, axis=-1), wo_ref[...])


def _l0_out_kernel(x_ref, up_ref, upp_ref, upn_ref, pw_ref, ps_ref, wa_ref, o_ref, wb_ref,
                   xg_ref, wq_ref, qg_ref, mk_ref, mv_ref, wo_ref, out_ref, ext_ref, *, tm, seq_len):
    a = _pool_mix(up_ref, upp_ref, upn_ref, pw_ref, ps_ref, ext_ref, tm, seq_len)
    x1 = x_ref[0] + _dot(a.astype(BF16), wa_ref[...]) + _dot(o_ref[0], wb_ref[...])
    out_ref[0] = _mem_xattn(x1, xg_ref, wq_ref, qg_ref, mk_ref, mv_ref, wo_ref)


def _l1_out_kernel(x_ref, o_ref, wb_ref, xg_ref, wq_ref, qg_ref, mk_ref, mv_ref, wo_ref, out_ref):
    x1 = x_ref[0] + _dot(o_ref[0], wb_ref[...])
    out_ref[0] = _mem_xattn(x1, xg_ref, wq_ref, qg_ref, mk_ref, mv_ref, wo_ref)


def _xattn_specs(n_mem):
    D = D_MODEL
    return [
        _const_spec((1, D)),
        _const_spec((D, D)),
        _const_spec((1, MEM_HEAD_DIM)),
        pl.BlockSpec((1, n_mem, D), lambda b, i: (b, 0, 0)),
        pl.BlockSpec((1, n_mem, D), lambda b, i: (b, 0, 0)),
        _const_spec((D, D)),
    ]


def _l0_out(x, up, pool_w, pool_scale, wa, o, wb, xg, wq, qg, mk, mv, wo):
    B, S, D = x.shape
    tm = TOKEN_TILE
    halo_blocks = tm // POOL_HALO
    tile = pl.BlockSpec((1, tm, D), lambda b, i: (b, i, 0))
    return pl.pallas_call(
        functools.partial(_l0_out_kernel, tm=tm, seq_len=S),
        grid=(B, S // tm),
        in_specs=[
            tile,
            pl.BlockSpec((1, tm, POOL_WIDTH), lambda b, i: (b, i, 0)),
            pl.BlockSpec((1, POOL_HALO, POOL_WIDTH),
                         lambda b, i: (b, jnp.maximum(i * halo_blocks - 1, 0), 0)),
            pl.BlockSpec((1, POOL_HALO, POOL_WIDTH),
                         lambda b, i: (b, jnp.minimum((i + 1) * halo_blocks, S // POOL_HALO - 1), 0)),
            _const_spec(pool_w.shape),
            _const_spec((1, POOL_WIDTH)),
            _const_spec((POOL_WIDTH, D)),
            pl.BlockSpec((1, tm, o.shape[-1]), lambda b, i: (b, i, 0)),
            _const_spec(wb.shape),
        ] + _xattn_specs(mk.shape[1]),
        out_specs=tile,
        out_shape=jax.ShapeDtypeStruct((B, S, D), F32),
        scratch_shapes=[pltpu.VMEM((tm + 2 * POOL_HALO, POOL_WIDTH), F32)],
        compiler_params=_params(),
        name="l0_out_xattn",
    )(x, up, up, up, pool_w, pool_scale, wa, o, wb, xg, wq, qg, mk, mv, wo)


def _l1_out(x, o, wb, xg, wq, qg, mk, mv, wo):
    B, S, D = x.shape
    tm = TOKEN_TILE
    tile = pl.BlockSpec((1, tm, D), lambda b, i: (b, i, 0))
    return pl.pallas_call(
        _l1_out_kernel,
        grid=(B, S // tm),
        in_specs=[tile, pl.BlockSpec((1, tm, o.shape[-1]), lambda b, i: (b, i, 0)), _const_spec(wb.shape)]
        + _xattn_specs(mk.shape[1]),
        out_specs=tile,
        out_shape=jax.ShapeDtypeStruct((B, S, D), F32),
        compiler_params=_params(),
        name="l1_out_xattn",
    )(x, o, wb, xg, wq, qg, mk, mv, wo)


def _mlp_kernel(x_ref, g_ref, w1_ref, w2_ref, out_ref, *, ff_chunk):
    x = x_ref[0]
    h = _rms(x, g_ref[...]).astype(BF16)
    acc = x
    for c in range(D_FF // ff_chunk):
        sl = slice(c * ff_chunk, (c + 1) * ff_chunk)
        a = jnp.maximum(_dot(h, w1_ref[:, sl]), 0.0)
        acc = acc + _dot((a * a).astype(BF16), w2_ref[sl, :])
    out_ref[0] = acc


def _mlp(x, g, w1, w2):
    B, S, D = x.shape
    tm = TOKEN_TILE
    tile = pl.BlockSpec((1, tm, D), lambda b, i: (b, i, 0))
    return pl.pallas_call(
        functools.partial(_mlp_kernel, ff_chunk=1024),
        grid=(B, S // tm),
        in_specs=[tile, _const_spec((1, D)), _const_spec((D, D_FF)), _const_spec((D_FF, D))],
        out_specs=tile,
        out_shape=jax.ShapeDtypeStruct((B, S, D), F32),
        compiler_params=_params(),
        name="mlp",
    )(x, g, w1, w2)


def _pair_norm(z, g2, lo_mask):
    y = z * z
    lo = jnp.sum(jnp.where(lo_mask, y, 0.0), axis=-1, keepdims=True)
    hi = jnp.sum(jnp.where(lo_mask, 0.0, y), axis=-1, keepdims=True)
    inv = 1.0 / NA_HEAD_DIM
    r = jnp.where(lo_mask, lax.rsqrt(lo * inv + EPS), lax.rsqrt(hi * inv + EPS))
    return z * r * g2


def _l1_front_kernel(x_ref, g_ref, w_ref, qg_ref, kg_ref, q_out, k_out, v_out):
    h = _rms(x_ref[0], g_ref[...]).astype(BF16)
    qkv = _dot(h, w_ref[...])
    width = NA_HEADS * NA_HEAD_DIM
    lo_mask = lax.broadcasted_iota(jnp.int32, (1, LANES), 1) < NA_HEAD_DIM
    scale = NA_HEAD_DIM ** -0.5
    for j in range(width // LANES):
        sl = slice(j * LANES, (j + 1) * LANES)
        q_out[0, :, sl] = (_pair_norm(qkv[:, sl], qg_ref[...], lo_mask) * scale).astype(BF16)
        ksl = slice(width + j * LANES, width + (j + 1) * LANES)
        k_out[0, :, sl] = _pair_norm(qkv[:, ksl], kg_ref[...], lo_mask).astype(BF16)
    v_out[0] = qkv[:, 2 * width:].astype(BF16)


def _l1_front(x, g, w_qkv, qg2, kg2):
    B, S, D = x.shape
    tm = TOKEN_TILE
    tile = pl.BlockSpec((1, tm, D), lambda b, i: (b, i, 0))
    return pl.pallas_call(
        _l1_front_kernel,
        grid=(B, S // tm),
        in_specs=[tile, _const_spec((1, D)), _const_spec((D, 3 * D)), _const_spec((1, LANES)),
                  _const_spec((1, LANES))],
        out_specs=[tile] * 3,
        out_shape=[jax.ShapeDtypeStruct((B, S, D), BF16)] * 3,
        compiler_params=_params(),
        name="l1_front",
    )(x, g, w_qkv, qg2, kg2)


def _na_kernel(q_ref, k_ref, v_ref, bias_ref, o_ref, *s_refs, rows_per_step, n_rows):
    i = pl.program_id(2)
    win = NA_KH * GRID_W
    lo_mask = lax.broadcasted_iota(jnp.int32, (1, LANES), 1) < NA_HEAD_DIM
    zero = jnp.zeros((), BF16)
    z_st = pl.multiple_of(jnp.minimum(pl.program_id(2), 0) * SUBLANES, SUBLANES)
    z_ld = pl.multiple_of(jnp.minimum(pl.program_id(1), 0) * SUBLANES, SUBLANES)
    st_rows, ld_rows = pl.ds(z_st, 2 * GRID_W), pl.ds(z_ld, 2 * GRID_W)

    def window(rr):
        r = i * rows_per_step + rr
        r0 = jnp.clip(r - NA_KH // 2, 0, n_rows - NA_KH)
        return r - r0, pl.ds(pl.multiple_of(r0 * GRID_W, GRID_W), win)

    def scores(rr):
        delta, keys = window(rr)
        q_row = q_ref[0, rr * GRID_W:(rr + 1) * GRID_W, :]
        q2 = jnp.concatenate([jnp.where(lo_mask, q_row, zero), jnp.where(lo_mask, zero, q_row)], axis=0)
        bias = jnp.concatenate([bias_ref[0, delta], bias_ref[1, delta]], axis=0)
        s_refs[rr][st_rows, :] = _dot_nt(q2, k_ref[0, keys, :]) + bias

    def attend(rr):
        _, keys = window(rr)
        s_ref = s_refs[rr]
        m = jnp.max(s_ref[ld_rows, :], axis=-1, keepdims=True)
        e = jnp.exp(s_ref[ld_rows, :] - m)
        denom = jnp.sum(e, axis=-1, keepdims=True)
        o2 = _dot(e.astype(BF16), v_ref[0, keys, :]) * (1.0 / denom)
        out = jnp.where(lo_mask, o2[:GRID_W], o2[GRID_W:])
        o_ref[0, rr * GRID_W:(rr + 1) * GRID_W, :] = out.astype(BF16)

    for rr in range(min(NA_ROWS_AHEAD, rows_per_step)):
        scores(rr)
    for rr in range(rows_per_step):
        if rr + NA_ROWS_AHEAD < rows_per_step:
            scores(rr + NA_ROWS_AHEAD)
        attend(rr)


def _na_attn(q, k, v, bias):
    B, S, D = q.shape
    n_rows = S // GRID_W
    tq = NA_ROWS_PER_STEP * GRID_W
    kv_spec = pl.BlockSpec((1, S, LANES), lambda b, j, i: (b, 0, j))
    return pl.pallas_call(
        functools.partial(_na_kernel, rows_per_step=NA_ROWS_PER_STEP, n_rows=n_rows),
        grid=(B, D // LANES, S // tq),
        in_specs=[
            pl.BlockSpec((1, tq, LANES), lambda b, j, i: (b, i, j)),
            kv_spec, kv_spec,
            pl.BlockSpec((2, NA_KH, GRID_W, NA_KH * GRID_W), lambda b, j, i: (j, 0, 0, 0)),
        ],
        out_specs=pl.BlockSpec((1, tq, LANES), lambda b, j, i: (b, i, j)),
        out_shape=jax.ShapeDtypeStruct((B, S, D), BF16),
        scratch_shapes=[pltpu.VMEM((2 * GRID_W + SUBLANES, NA_KH * GRID_W), F32)] * NA_ROWS_PER_STEP,
        compiler_params=_params(),
        name="na_attn",
    )(q, k, v, bias)


def _na_bias_table(rpb):
    cols = jnp.arange(GRID_W)
    c0 = jnp.clip(cols - NA_KW // 2, 0, GRID_W - NA_KW)
    kc = jnp.arange(GRID_W)
    valid = (kc[None, :] >= c0[:, None]) & (kc[None, :] < c0[:, None] + NA_KW)
    dc = jnp.clip(kc[None, :] - cols[:, None] + (NA_KW - 1), 0, 2 * NA_KW - 2)
    t = jnp.where(valid[None, None], rpb[:, :, dc], MASK_VALUE)
    per_delta = []
    for delta in range(NA_KH):
        rows = t[:, NA_KH - 1 - delta:2 * NA_KH - 1 - delta]
        per_delta.append(rows.transpose(0, 2, 1, 3).reshape(rpb.shape[0], GRID_W, NA_KH * GRID_W))
    return jnp.stack(per_delta, axis=1)


def _pad_heads(w, heads, width):
    lead = w.shape[:-1]
    w = w.reshape(*lead, heads, width)
    w = jnp.pad(w, [(0, 0)] * len(lead) + [(0, 0), (0, LANES - width)])
    return w.reshape(*lead, heads * LANES)


def _rope_tables(seq_len):
    half = MLA_ROPE // 2
    pos = jnp.arange(seq_len, dtype=F32)
    freqs = ROPE_THETA ** (-jnp.arange(half, dtype=F32) / half)
    ang = pos[:, None] * freqs[None, :]
    cos, sin = jnp.cos(ang), jnp.sin(ang)
    z = lambda n: jnp.zeros((seq_len, n), F32)
    cos_t = jnp.concatenate([jnp.ones((seq_len, MLA_NOPE), F32), cos, cos, z(LANES - MLA_QK)], axis=1)
    sin_a = jnp.concatenate([z(MLA_NOPE), -sin, z(half), z(LANES - MLA_QK)], axis=1)
    sin_b = jnp.concatenate([z(MLA_NOPE), z(half), sin, z(LANES - MLA_QK)], axis=1)
    return cos_t, sin_a, sin_b, cos.T, sin.T


def kernel(x, mem, mix_norm_g, xattn_norm_g, ff_norm_g, w_mem_q, mem_q_g, w_mem_o, w_ff1, w_ff2, mem_tok_norm_g, w_mem_kv, mem_k_g, w_in_e, pool_w, pool_scale, q_lora_g, w_uq, kv_lora_g, w_ukv, mla_q_g, mla_k_g, w_out_e, w_qkv_o, na_q_g, na_k_g, na_rpb, w_out_o):
    B, S, D = x.shape
    row = lambda v: v.reshape(1, -1)

    mem_k, mem_v = _mem_kv(mem, row(mem_tok_norm_g), w_mem_kv.astype(BF16), row(mem_k_g))

    o1 = POOL_WIDTH
    o3 = o1 + Q_LORA + KV_LORA
    w_in = w_in_e[0]
    w_in_pad = jnp.concatenate(
        [w_in[:, :o3], jnp.zeros((D, MLA_NOPE), F32), w_in[:, o3:], jnp.zeros((D, LANES - MLA_QK), F32)],
        axis=1).astype(BF16)
    wuqt = _pad_heads(w_uq[0], MLA_HEADS, MLA_QK).T.astype(BF16)
    w_ukv_h = w_ukv[0].reshape(KV_LORA, MLA_HEADS, MLA_NOPE + MLA_V)
    wuk = _pad_heads(w_ukv_h[:, :, :MLA_NOPE].reshape(KV_LORA, -1), MLA_HEADS, MLA_NOPE).astype(BF16)
    wuvt = _pad_heads(w_ukv_h[:, :, MLA_NOPE:].reshape(KV_LORA, -1), MLA_HEADS, MLA_V).T.astype(BF16)
    qg = jnp.pad(mla_q_g[0], (0, LANES - MLA_QK)) * (MLA_QK ** -0.5 * LOG2_E)
    qg = jnp.broadcast_to(qg[:, None], (LANES, LANES))
    kg = jnp.pad(mla_k_g[0], (0, LANES - MLA_QK)).reshape(1, LANES)
    up, q, k, v = _l0_front(x, row(mix_norm_g[0]), w_in_pad, row(q_lora_g[0]), wuqt, row(kv_lora_g[0]),
                            wuk, wuvt, qg, kg, *_rope_tables(S))
    o = _mla_attn(q, k, v)
    w_out = w_out_e[0]
    wa = w_out[:POOL_WIDTH].astype(BF16)
    wb = jnp.pad(w_out[POOL_WIDTH:].reshape(MLA_HEADS, MLA_V, D), ((0, 0), (0, LANES - MLA_V), (0, 0)))
    wb = wb.reshape(MLA_HEADS * LANES, D).astype(BF16)
    x = _l0_out(x, up, pool_w[0].astype(BF16), row(pool_scale[0]), wa, o, wb,
                row(xattn_norm_g[0]), w_mem_q[0].astype(BF16), row(mem_q_g[0]), mem_k, mem_v,
                w_mem_o[0].astype(BF16))
    x = _mlp(x, row(ff_norm_g[0]), w_ff1[0].astype(BF16), w_ff2[0].astype(BF16))

    qg2 = jnp.tile(na_q_g[0], LANES // NA_HEAD_DIM).reshape(1, LANES)
    kg2 = jnp.tile(na_k_g[0], LANES // NA_HEAD_DIM).reshape(1, LANES)
    q, k, v = _l1_front(x, row(mix_norm_g[1]), w_qkv_o[0].astype(BF16), qg2, kg2)
    c = _na_attn(q, k, v, _na_bias_table(na_rpb[0]))
    x = _l1_out(x, c, w_out_o[0].astype(BF16), row(xattn_norm_g[1]), w_mem_q[1].astype(BF16),
                row(mem_q_g[1]), mem_k, mem_v, w_mem_o[1].astype(BF16))
    x = _mlp(x, row(ff_norm_g[1]), w_ff1[1].astype(BF16), w_ff2[1].astype(BF16))
    return x
```

```python
import functools

import jax
import jax.numpy as jnp
from jax import lax
from jax.experimental import pallas as pl
from jax.experimental.pallas import tpu as pltpu

F32 = jnp.float32
BF16 = jnp.bfloat16

D_MODEL = 1024
GRID_W = 64
EPS = 1e-6
POOL_WIDTH = 512
POOL_GROUP_W = 128
POOL_WINDOWS = (2, 4, 8, 16)
POOL_HALO = 8
MLA_HEADS = 8
MLA_NOPE = 64
MLA_ROPE = 32
MLA_V = 64
MLA_QK = MLA_NOPE + MLA_ROPE
Q_LORA = 256
KV_LORA = 128
ROPE_THETA = 10000.0
NA_HEADS = 16
NA_HEAD_DIM = 64
NA_KH = 8
NA_KW = 16
MEM_HEADS = 4
MEM_HEAD_DIM = 256
D_FF = 4096

LANES = 128
SUBLANES = 8
MASK_VALUE = -1e30
LOG2_E = 1.4426950408889634
VMEM_LIMIT = 48 * 1024 * 1024

TOKEN_TILE = 512
OUT_SUB_TILES = 2
MLA_Q_TILE = 512
MLA_KV_TILE = 512
MLA_Q_TILES_PER_STEP = 2
MLA_VT_ROWS = 80
NA_ROWS_PER_STEP = 16
NA_ROWS_AHEAD = 4


def _const_spec(shape):
    zeros = (0,) * len(shape)
    return pl.BlockSpec(shape, lambda *_: zeros, pipeline_mode=pl.Buffered(1))


def _params(**flags):
    return pltpu.CompilerParams(vmem_limit_bytes=VMEM_LIMIT, flags=flags or None)


def _rms(x, g, n=None):
    n = x.shape[-1] if n is None else n
    ms = jnp.sum(x * x, axis=-1, keepdims=True) * (1.0 / n)
    return x * lax.rsqrt(ms + EPS) * g


def _dot(a, b):
    return jnp.dot(a, b, preferred_element_type=F32)


def _dot_nt(a, b):
    return lax.dot_general(a, b, (((1,), (1,)), ((), ())), preferred_element_type=F32)


def _mem_kv_kernel(mem_ref, g_ref, w_ref, kg_ref, k_out, v_out):
    mn = _rms(mem_ref[0], g_ref[...]).astype(BF16)
    kv = _dot(mn, w_ref[...])
    for h in range(MEM_HEADS):
        sl = slice(h * MEM_HEAD_DIM, (h + 1) * MEM_HEAD_DIM)
        k_out[0, :, sl] = _rms(kv[:, sl], kg_ref[...]).astype(BF16)
    v_out[0] = kv[:, D_MODEL:].astype(BF16)


def _mem_kv(mem, g, w_kv, k_g):
    B, N, D = mem.shape
    return pl.pallas_call(
        _mem_kv_kernel,
        grid=(B,),
        in_specs=[
            pl.BlockSpec((1, N, D), lambda b: (b, 0, 0)),
            _const_spec((1, D)),
            _const_spec((D, 2 * D)),
            _const_spec((1, MEM_HEAD_DIM)),
        ],
        out_specs=[pl.BlockSpec((1, N, D), lambda b: (b, 0, 0))] * 2,
        out_shape=[jax.ShapeDtypeStruct((B, N, D), BF16)] * 2,
        compiler_params=_params(),
        name="mem_kv",
    )(mem, g, w_kv, k_g)


def _l0_front_kernel(x_ref, g_ref, win_ref, qlg_ref, wuqt_ref, kvlg_ref, wuk_ref, wuvt_ref,
                     qg_ref, kg_ref, cos_ref, sa_ref, sb_ref, cost_ref, sint_ref,
                     up_ref, q_ref, k_ref, v_ref):
    tm = x_ref.shape[1]
    h = _rms(x_ref[0], g_ref[...]).astype(BF16)
    u = _dot(h, win_ref[...])
    up_ref[0] = u[:, :POOL_WIDTH]
    o2 = POOL_WIDTH + Q_LORA
    o3 = o2 + KV_LORA
    cq = _rms(u[:, POOL_WIDTH:o2], qlg_ref[...])
    ckv = _rms(u[:, o2:o3], kvlg_ref[...])
    kr = u[:, o3:]
    qt = _dot(wuqt_ref[...], cq.T.astype(BF16))
    vt = _dot(wuvt_ref[...], ckv.T.astype(BF16))
    kf = _dot(ckv.astype(BF16), wuk_ref[...])

    half = MLA_ROPE // 2
    r1 = slice(MLA_NOPE, MLA_NOPE + half)
    r2 = slice(MLA_NOPE + half, MLA_QK)
    cos_t, sin_t = cost_ref[...], sint_ref[...]
    gq = jnp.tile(qg_ref[...], (1, tm // LANES))
    row_id = lax.broadcasted_iota(jnp.int32, (LANES, tm), 0)
    inv_n = 1.0 / MLA_QK
    for hd in range(MLA_HEADS):
        rows = slice(hd * LANES, (hd + 1) * LANES)
        blk = qt[rows]
        ms = jnp.sum(blk * blk, axis=0, keepdims=True) * inv_n
        y = blk * lax.rsqrt(ms + EPS) * gq
        y1, y2 = y[r1], y[r2]
        q_ref[0, hd] = jnp.concatenate(
            [y[:MLA_NOPE], y1 * cos_t - y2 * sin_t, y1 * sin_t + y2 * cos_t, y[MLA_QK:]], axis=0).astype(BF16)
        v_ref[0, hd] = jnp.where(row_id == MLA_V, 1.0, vt[rows]).astype(BF16)

    krg = kr * kg_ref[...]
    partner = pltpu.roll(krg, LANES - half, 1) * sa_ref[...] + pltpu.roll(krg, half, 1) * sb_ref[...]
    gcos = kg_ref[...] * cos_ref[...]
    for hd in range(MLA_HEADS):
        pre = kf[:, hd * LANES:(hd + 1) * LANES] + kr
        ms = jnp.sum(pre * pre, axis=-1, keepdims=True) * inv_n
        k_ref[0, hd] = ((pre * gcos + partner) * lax.rsqrt(ms + EPS)).astype(BF16)


def _l0_front(x, g, w_in, qlg, wuqt, kvlg, wuk, wuvt, qg, kg, cos, sa, sb, cos_t, sin_t):
    B, S, D = x.shape
    tm = TOKEN_TILE
    hw = MLA_HEADS * LANES
    head_spec = pl.BlockSpec((1, MLA_HEADS, tm, LANES), lambda b, i: (b, 0, i, 0))
    head_t_spec = pl.BlockSpec((1, MLA_HEADS, LANES, tm), lambda b, i: (b, 0, 0, i))
    tab_spec = pl.BlockSpec((tm, LANES), lambda b, i: (i, 0))
    tab_t_spec = pl.BlockSpec((MLA_ROPE // 2, tm), lambda b, i: (0, i))
    return pl.pallas_call(
        _l0_front_kernel,
        grid=(B, S // tm),
        in_specs=[
            pl.BlockSpec((1, tm, D), lambda b, i: (b, i, 0)),
            _const_spec((1, D)),
            _const_spec((D, D)),
            _const_spec((1, Q_LORA)),
            _const_spec((hw, Q_LORA)),
            _const_spec((1, KV_LORA)),
            _const_spec((KV_LORA, hw)),
            _const_spec((hw, KV_LORA)),
            _const_spec((LANES, LANES)),
            _const_spec((1, LANES)),
            tab_spec, tab_spec, tab_spec, tab_t_spec, tab_t_spec,
        ],
        out_specs=[
            pl.BlockSpec((1, tm, POOL_WIDTH), lambda b, i: (b, i, 0)),
            head_t_spec, head_spec, head_t_spec,
        ],
        out_shape=[
            jax.ShapeDtypeStruct((B, S, POOL_WIDTH), F32),
            jax.ShapeDtypeStruct((B, MLA_HEADS, LANES, S), BF16),
            jax.ShapeDtypeStruct((B, MLA_HEADS, S, LANES), BF16),
            jax.ShapeDtypeStruct((B, MLA_HEADS, LANES, S), BF16),
        ],
        compiler_params=_params(),
        name="l0_front",
    )(x, g, w_in, qlg, wuqt, kvlg, wuk, wuvt, qg, kg, cos, sa, sb, cos_t, sin_t)


def _mla_attn_kernel(qt_ref, k_ref, vt_ref, o_ref, s0_ref, s1_ref, s2_ref, acc0_ref, acc1_ref, *, tq, tk):
    nk = k_ref.shape[2] // tk
    nq = qt_ref.shape[3] // tq
    bufs = (s0_ref, s1_ref, s2_ref)
    accs = (acc0_ref, acc1_ref)
    z_st = pl.multiple_of(jnp.minimum(pl.program_id(2), 0) * SUBLANES, SUBLANES)
    z_ld = pl.multiple_of(jnp.minimum(pl.program_id(1), 0) * SUBLANES, SUBLANES)
    st_rows, ld_rows = pl.ds(z_st, tk), pl.ds(z_ld, tk)
    acc_st, acc_ld = pl.ds(z_st, MLA_VT_ROWS), pl.ds(z_ld, MLA_VT_ROWS)

    def fill(v):
        t, c = divmod(v, nk)
        qt = qt_ref[0, 0, :, t * tq:(t + 1) * tq]
        bufs[v % 3][st_rows, :] = _dot(k_ref[0, 0, c * tk:(c + 1) * tk, :], qt)

    def update(v, m_old):
        t, c = divmod(v, nk)
        s_ref, acc_ref = bufs[v % 3], accs[t % 2]
        vt = vt_ref[0, 0, :MLA_VT_ROWS, c * tk:(c + 1) * tk]
        cm = jnp.max(s_ref[ld_rows, :], axis=0, keepdims=True)
        m_new = cm if c == 0 else jnp.maximum(m_old, cm)
        p = jnp.exp2((s_ref[ld_rows, :] - m_new).astype(BF16))
        pv = _dot(vt, p)
        if c == 0:
            acc_ref[acc_st, :] = pv
        else:
            acc_ref[acc_st, :] = acc_ref[acc_ld, :] * jnp.exp2(m_old - m_new) + pv
        if c == nk - 1:
            acc = acc_ref[acc_ld, :]
            out = acc[:MLA_V] * (1.0 / acc[MLA_V:MLA_V + 1])
            out = jnp.concatenate([out, jnp.zeros((LANES - MLA_V, tq), F32)], axis=0)
            o_ref[0, t * tq:(t + 1) * tq, :] = out.T.astype(BF16)
        return m_new

    fill(0)
    fill(1)
    m = None
    for v in range(nq * nk):
        if v + 2 < nq * nk:
            fill(v + 2)
        m = update(v, m)


def _mla_attn(qt, k, vt):
    B, H, S, L = k.shape
    tq, tk, nq = MLA_Q_TILE, MLA_KV_TILE, MLA_Q_TILES_PER_STEP
    assert S % tk == 0 and S % (nq * tq) == 0
    return pl.pallas_call(
        functools.partial(_mla_attn_kernel, tq=tq, tk=tk),
        grid=(B, H, S // (nq * tq)),
        in_specs=[
            pl.BlockSpec((1, 1, L, nq * tq), lambda b, h, i: (b, h, 0, i)),
            pl.BlockSpec((1, 1, S, L), lambda b, h, i: (b, h, 0, 0)),
            pl.BlockSpec((1, 1, L, S), lambda b, h, i: (b, h, 0, 0)),
        ],
        out_specs=pl.BlockSpec((1, nq * tq, L), lambda b, h, i: (b, i, h)),
        out_shape=jax.ShapeDtypeStruct((B, S, H * L), BF16),
        scratch_shapes=[pltpu.VMEM((tk + SUBLANES, tq), F32)] * 3
        + [pltpu.VMEM((MLA_VT_ROWS + SUBLANES, tq), F32)] * 2,
        compiler_params=_params(),
        name="mla_attn",
    )(qt, k, vt)


def _pool_mix(up_ref, upp_ref, upn_ref, pw_ref, ps_ref, ext_ref, tm, seq_len):
    i = pl.program_id(1)
    last = pl.num_programs(1) - 1
    ext_ref[0:POOL_HALO] = jnp.where(i > 0, upp_ref[0], 0.0)
    ext_ref[POOL_HALO:POOL_HALO + tm] = up_ref[0]
    ext_ref[POOL_HALO + tm:2 * POOL_HALO + tm] = jnp.where(i < last, upn_ref[0], 0.0)
    t = i * tm + lax.broadcasted_iota(jnp.int32, (tm, 1), 0)
    outs = []
    for g, w in enumerate(POOL_WINDOWS):
        cols = slice(g * POOL_GROUP_W, (g + 1) * POOL_GROUP_W)
        wsum = None
        for o in range(-(w // 2), w - w // 2):
            piece = ext_ref[POOL_HALO + o:POOL_HALO + o + tm, cols]
            wsum = piece if wsum is None else wsum + piece
        lo = jnp.maximum(t - w // 2, 0)
        hi = jnp.minimum(t + (w - 1 - w // 2), seq_len - 1)
        cnt = (hi - lo + 1).astype(F32)
        d = wsum / cnt - ext_ref[POOL_HALO:POOL_HALO + tm, cols]
        outs.append(_dot(d.astype(BF16), pw_ref[g]))
    return jnp.concatenate(outs, axis=-1) * ps_ref[...]


def _mem_xattn(x1s, xg_ref, wq_ref, qg_ref, mk_ref, mv_ref, wo_ref):
    scale = MEM_HEAD_DIM ** -0.5
    heads = [slice(hd * MEM_HEAD_DIM, (hd + 1) * MEM_HEAD_DIM) for hd in range(MEM_HEADS)]
    qms = [_dot(_rms(x1, xg_ref[...]).astype(BF16), wq_ref[...]) for x1 in x1s]
    scores = [[_dot_nt((_rms(qm[:, sl], qg_ref[...]) * scale).astype(BF16), mk_ref[0, :, sl]) for sl in heads]
              for qm in qms]
    outs = []
    for x1, per_head in zip(x1s, scores):
        ohs = []
        for s, sl in zip(per_head, heads):
            e = jnp.exp(s - jnp.max(s, axis=-1, keepdims=True))
            denom = jnp.sum(e, axis=-1, keepdims=True)
            ohs.append((_dot(e.astype(BF16), mv_ref[0, :, sl]) * (1.0 / denom)).astype(BF16))
        outs.append(x1 + _dot(jnp.concatenate(ohs, axis=-1), wo_ref[...]))
    return outs


def _sub_tiles(tm):
    step = tm // OUT_SUB_TILES
    return [slice(r * step, (r + 1) * step) for r in range(OUT_SUB_TILES)]


def _l0_out_kernel(x_ref, up_ref, upp_ref, upn_ref, pw_ref, ps_ref, wa_ref, o_ref, wb_ref,
                   xg_ref, wq_ref, qg_ref, mk_ref, mv_ref, wo_ref, out_ref, ext_ref, *, tm, seq_len):
    a = _pool_mix(up_ref, upp_ref, upn_ref, pw_ref, ps_ref, ext_ref, tm, seq_len).astype(BF16)
    subs = _sub_tiles(tm)
    x1s = [x_ref[0, rows] + _dot(a[rows], wa_ref[...]) + _dot(o_ref[0, rows], wb_ref[...]) for rows in subs]
    for rows, out in zip(subs, _mem_xattn(x1s, xg_ref, wq_ref, qg_ref, mk_ref, mv_ref, wo_ref)):
        out_ref[0, rows] = out


def _l1_out_kernel(x_ref, o_ref, wb_ref, xg_ref, wq_ref, qg_ref, mk_ref, mv_ref, wo_ref, out_ref, *, tm):
    subs = _sub_tiles(tm)
    x1s = [x_ref[0, rows] + _dot(o_ref[0, rows], wb_ref[...]) for rows in subs]
    for rows, out in zip(subs, _mem_xattn(x1s, xg_ref, wq_ref, qg_ref, mk_ref, mv_ref, wo_ref)):
        out_ref[0, rows] = out


def _xattn_specs(n_mem):
    D = D_MODEL
    return [
        _const_spec((1, D)),
        _const_spec((D, D)),
        _const_spec((1, MEM_HEAD_DIM)),
        pl.BlockSpec((1, n_mem, D), lambda b, i: (b, 0, 0)),
        pl.BlockSpec((1, n_mem, D), lambda b, i: (b, 0, 0)),
        _const_spec((D, D)),
    ]


def _l0_out(x, up, pool_w, pool_scale, wa, o, wb, xg, wq, qg, mk, mv, wo):
    B, S, D = x.shape
    tm = TOKEN_TILE
    halo_blocks = tm // POOL_HALO
    tile = pl.BlockSpec((1, tm, D), lambda b, i: (b, i, 0))
    return pl.pallas_call(
        functools.partial(_l0_out_kernel, tm=tm, seq_len=S),
        grid=(B, S // tm),
        in_specs=[
            tile,
            pl.BlockSpec((1, tm, POOL_WIDTH), lambda b, i: (b, i, 0)),
            pl.BlockSpec((1, POOL_HALO, POOL_WIDTH),
                         lambda b, i: (b, jnp.maximum(i * halo_blocks - 1, 0), 0)),
            pl.BlockSpec((1, POOL_HALO, POOL_WIDTH),
                         lambda b, i: (b, jnp.minimum((i + 1) * halo_blocks, S // POOL_HALO - 1), 0)),
            _const_spec(pool_w.shape),
            _const_spec((1, POOL_WIDTH)),
            _const_spec((POOL_WIDTH, D)),
            pl.BlockSpec((1, tm, o.shape[-1]), lambda b, i: (b, i, 0)),
            _const_spec(wb.shape),
        ] + _xattn_specs(mk.shape[1]),
        out_specs=tile,
        out_shape=jax.ShapeDtypeStruct((B, S, D), F32),
        scratch_shapes=[pltpu.VMEM((tm + 2 * POOL_HALO, POOL_WIDTH), F32)],
        compiler_params=_params(),
        name="l0_out_xattn",
    )(x, up, up, up, pool_w, pool_scale, wa, o, wb, xg, wq, qg, mk, mv, wo)


def _l1_out(x, o, wb, xg, wq, qg, mk, mv, wo):
    B, S, D = x.shape
    tm = TOKEN_TILE
    tile = pl.BlockSpec((1, tm, D), lambda b, i: (b, i, 0))
    return pl.pallas_call(
        functools.partial(_l1_out_kernel, tm=tm),
        grid=(B, S // tm),
        in_specs=[tile, pl.BlockSpec((1, tm, o.shape[-1]), lambda b, i: (b, i, 0)), _const_spec(wb.shape)]
        + _xattn_specs(mk.shape[1]),
        out_specs=tile,
        out_shape=jax.ShapeDtypeStruct((B, S, D), F32),
        compiler_params=_params(),
        name="l1_out_xattn",
    )(x, o, wb, xg, wq, qg, mk, mv, wo)


def _mlp_kernel(x_ref, g_ref, w1_ref, w2_ref, out_ref, *, ff_chunk):
    x = x_ref[0]
    h = _rms(x, g_ref[...]).astype(BF16)
    acc = x
    for c in range(D_FF // ff_chunk):
        sl = slice(c * ff_chunk, (c + 1) * ff_chunk)
        a = jnp.maximum(_dot(h, w1_ref[:, sl]), 0.0)
        acc = acc + _dot((a * a).astype(BF16), w2_ref[sl, :])
    out_ref[0] = acc


def _mlp(x, g, w1, w2):
    B, S, D = x.shape
    tm = TOKEN_TILE
    tile = pl.BlockSpec((1, tm, D), lambda b, i: (b, i, 0))
    return pl.pallas_call(
        functools.partial(_mlp_kernel, ff_chunk=1024),
        grid=(B, S // tm),
        in_specs=[tile, _const_spec((1, D)), _const_spec((D, D_FF)), _const_spec((D_FF, D))],
        out_specs=tile,
        out_shape=jax.ShapeDtypeStruct((B, S, D), F32),
        compiler_params=_params(),
        name="mlp",
    )(x, g, w1, w2)


def _pair_norm(z, g2, lo_mask):
    y = z * z
    lo = jnp.sum(jnp.where(lo_mask, y, 0.0), axis=-1, keepdims=True)
    hi = jnp.sum(jnp.where(lo_mask, 0.0, y), axis=-1, keepdims=True)
    inv = 1.0 / NA_HEAD_DIM
    r = jnp.where(lo_mask, lax.rsqrt(lo * inv + EPS), lax.rsqrt(hi * inv + EPS))
    return z * r * g2


def _l1_front_kernel(x_ref, g_ref, w_ref, qg_ref, kg_ref, q_out, k_out, v_out):
    h = _rms(x_ref[0], g_ref[...]).astype(BF16)
    qkv = _dot(h, w_ref[...])
    width = NA_HEADS * NA_HEAD_DIM
    lo_mask = lax.broadcasted_iota(jnp.int32, (1, LANES), 1) < NA_HEAD_DIM
    scale = NA_HEAD_DIM ** -0.5
    for j in range(width // LANES):
        sl = slice(j * LANES, (j + 1) * LANES)
        q_out[0, :, sl] = (_pair_norm(qkv[:, sl], qg_ref[...], lo_mask) * scale).astype(BF16)
        ksl = slice(width + j * LANES, width + (j + 1) * LANES)
        k_out[0, :, sl] = _pair_norm(qkv[:, ksl], kg_ref[...], lo_mask).astype(BF16)
    v_out[0] = qkv[:, 2 * width:].astype(BF16)


def _l1_front(x, g, w_qkv, qg2, kg2):
    B, S, D = x.shape
    tm = TOKEN_TILE
    tile = pl.BlockSpec((1, tm, D), lambda b, i: (b, i, 0))
    return pl.pallas_call(
        _l1_front_kernel,
        grid=(B, S // tm),
        in_specs=[tile, _const_spec((1, D)), _const_spec((D, 3 * D)), _const_spec((1, LANES)),
                  _const_spec((1, LANES))],
        out_specs=[tile] * 3,
        out_shape=[jax.ShapeDtypeStruct((B, S, D), BF16)] * 3,
        compiler_params=_params(),
        name="l1_front",
    )(x, g, w_qkv, qg2, kg2)


def _na_bias_tiles(rpb_ref, tab_ref):
    c = lax.broadcasted_iota(jnp.int32, (GRID_W, LANES), 0)
    kc = lax.broadcasted_iota(jnp.int32, (GRID_W, LANES), 1) % GRID_W
    c0 = jnp.clip(c - NA_KW // 2, 0, GRID_W - NA_KW)
    valid = (kc >= c0) & (kc < c0 + NA_KW)
    for h in range(tab_ref.shape[0]):
        for dr in range(tab_ref.shape[1]):
            rows = jnp.broadcast_to(rpb_ref[h, dr:dr + 1, :], (GRID_W, LANES))
            skew = pltpu.roll(rows, LANES - (NA_KW - 1), 1, stride=1, stride_axis=0)
            tab_ref[h, dr] = jnp.where(valid, skew, MASK_VALUE)


def _na_kernel(q_ref, k_ref, v_ref, rpb_ref, o_ref, tab_ref, *s_refs, rows_per_step, n_rows):
    i = pl.program_id(2)
    win = NA_KH * GRID_W
    lo_mask = lax.broadcasted_iota(jnp.int32, (1, LANES), 1) < NA_HEAD_DIM
    zero = jnp.zeros((), BF16)

    @pl.when(i == 0)
    def _():
        _na_bias_tiles(rpb_ref, tab_ref)

    z_st = pl.multiple_of(jnp.minimum(pl.program_id(2), 0) * SUBLANES, SUBLANES)
    z_ld = pl.multiple_of(jnp.minimum(pl.program_id(1), 0) * SUBLANES, SUBLANES)
    st_rows, ld_rows = pl.ds(z_st, 2 * GRID_W), pl.ds(z_ld, 2 * GRID_W)

    def window(rr):
        r = i * rows_per_step + rr
        r0 = jnp.clip(r - NA_KH // 2, 0, n_rows - NA_KH)
        return r - r0, pl.ds(pl.multiple_of(r0 * GRID_W, GRID_W), win)

    def scores(rr):
        delta, keys = window(rr)
        q_row = q_ref[0, rr * GRID_W:(rr + 1) * GRID_W, :]
        q2 = jnp.concatenate([jnp.where(lo_mask, q_row, zero), jnp.where(lo_mask, zero, q_row)], axis=0)
        bias = jnp.concatenate(
            [jnp.concatenate([tab_ref[h, 2 * p + (NA_KH - 1) - delta] for p in range(NA_KH // 2)], axis=1)
             for h in range(2)], axis=0)
        s_refs[rr][st_rows, :] = _dot_nt(q2, k_ref[0, keys, :]) + bias

    def attend(rr):
        _, keys = window(rr)
        s_ref = s_refs[rr]
        m = jnp.max(s_ref[ld_rows, :], axis=-1, keepdims=True)
        e = jnp.exp(s_ref[ld_rows, :] - m)
        denom = jnp.sum(e, axis=-1, keepdims=True)
        o2 = _dot(e.astype(BF16), v_ref[0, keys, :]) * (1.0 / denom)
        out = jnp.where(lo_mask, o2[:GRID_W], o2[GRID_W:])
        o_ref[0, rr * GRID_W:(rr + 1) * GRID_W, :] = out.astype(BF16)

    for rr in range(min(NA_ROWS_AHEAD, rows_per_step)):
        scores(rr)
    for rr in range(rows_per_step):
        if rr + NA_ROWS_AHEAD < rows_per_step:
            scores(rr + NA_ROWS_AHEAD)
        attend(rr)


def _na_attn(q, k, v, rpb_pairs):
    B, S, D = q.shape
    n_rows = S // GRID_W
    tq = NA_ROWS_PER_STEP * GRID_W
    n_dr = rpb_pairs.shape[1]
    kv_spec = pl.BlockSpec((1, S, LANES), lambda b, j, i: (b, 0, j))
    return pl.pallas_call(
        functools.partial(_na_kernel, rows_per_step=NA_ROWS_PER_STEP, n_rows=n_rows),
        grid=(B, D // LANES, S // tq),
        in_specs=[
            pl.BlockSpec((1, tq, LANES), lambda b, j, i: (b, i, j)),
            kv_spec, kv_spec,
            pl.BlockSpec((2, n_dr, LANES), lambda b, j, i: (j, 0, 0)),
        ],
        out_specs=pl.BlockSpec((1, tq, LANES), lambda b, j, i: (b, i, j)),
        out_shape=jax.ShapeDtypeStruct((B, S, D), BF16),
        scratch_shapes=[pltpu.VMEM((2, n_dr, GRID_W, LANES), F32)]
        + [pltpu.VMEM((2 * GRID_W + SUBLANES, NA_KH * GRID_W), F32)] * NA_ROWS_PER_STEP,
        compiler_params=_params(),
        name="na_attn",
    )(q, k, v, rpb_pairs)


def _na_rpb_pairs(rpb):
    padded = jnp.pad(rpb, ((0, 0), (0, 0), (0, GRID_W - rpb.shape[-1])))
    return jnp.concatenate([padded[:, :-1], padded[:, 1:]], axis=-1)


def _pad_heads(w, heads, width):
    lead = w.shape[:-1]
    w = w.reshape(*lead, heads, width)
    w = jnp.pad(w, [(0, 0)] * len(lead) + [(0, 0), (0, LANES - width)])
    return w.reshape(*lead, heads * LANES)


def _rope_tables(seq_len):
    half = MLA_ROPE // 2
    pos = jnp.arange(seq_len, dtype=F32)
    freqs = ROPE_THETA ** (-jnp.arange(half, dtype=F32) / half)
    ang = pos[:, None] * freqs[None, :]
    cos, sin = jnp.cos(ang), jnp.sin(ang)
    z = lambda n: jnp.zeros((seq_len, n), F32)
    cos_t = jnp.concatenate([jnp.ones((seq_len, MLA_NOPE), F32), cos, cos, z(LANES - MLA_QK)], axis=1)
    sin_a = jnp.concatenate([z(MLA_NOPE), -sin, z(half), z(LANES - MLA_QK)], axis=1)
    sin_b = jnp.concatenate([z(MLA_NOPE), z(half), sin, z(LANES - MLA_QK)], axis=1)
    return cos_t, sin_a, sin_b, cos.T, sin.T


def kernel(x, mem, mix_norm_g, xattn_norm_g, ff_norm_g, w_mem_q, mem_q_g, w_mem_o, w_ff1, w_ff2, mem_tok_norm_g, w_mem_kv, mem_k_g, w_in_e, pool_w, pool_scale, q_lora_g, w_uq, kv_lora_g, w_ukv, mla_q_g, mla_k_g, w_out_e, w_qkv_o, na_q_g, na_k_g, na_rpb, w_out_o):
    B, S, D = x.shape
    row = lambda v: v.reshape(1, -1)

    mem_k, mem_v = _mem_kv(mem, row(mem_tok_norm_g), w_mem_kv.astype(BF16), row(mem_k_g))

    o1 = POOL_WIDTH
    o3 = o1 + Q_LORA + KV_LORA
    w_in = w_in_e[0]
    w_in_pad = jnp.concatenate(
        [w_in[:, :o3], jnp.zeros((D, MLA_NOPE), F32), w_in[:, o3:], jnp.zeros((D, LANES - MLA_QK), F32)],
        axis=1).astype(BF16)
    wuqt = _pad_heads(w_uq[0], MLA_HEADS, MLA_QK).T.astype(BF16)
    w_ukv_h = w_ukv[0].reshape(KV_LORA, MLA_HEADS, MLA_NOPE + MLA_V)
    wuk = _pad_heads(w_ukv_h[:, :, :MLA_NOPE].reshape(KV_LORA, -1), MLA_HEADS, MLA_NOPE).astype(BF16)
    wuvt = _pad_heads(w_ukv_h[:, :, MLA_NOPE:].reshape(KV_LORA, -1), MLA_HEADS, MLA_V).T.astype(BF16)
    qg = jnp.pad(mla_q_g[0], (0, LANES - MLA_QK)) * (MLA_QK ** -0.5 * LOG2_E)
    qg = jnp.broadcast_to(qg[:, None], (LANES, LANES))
    kg = jnp.pad(mla_k_g[0], (0, LANES - MLA_QK)).reshape(1, LANES)
    up, q, k, v = _l0_front(x, row(mix_norm_g[0]), w_in_pad, row(q_lora_g[0]), wuqt, row(kv_lora_g[0]),
                            wuk, wuvt, qg, kg, *_rope_tables(S))
    o = _mla_attn(q, k, v)
    w_out = w_out_e[0]
    wa = w_out[:POOL_WIDTH].astype(BF16)
    wb = jnp.pad(w_out[POOL_WIDTH:].reshape(MLA_HEADS, MLA_V, D), ((0, 0), (0, LANES - MLA_V), (0, 0)))
    wb = wb.reshape(MLA_HEADS * LANES, D).astype(BF16)
    x = _l0_out(x, up, pool_w[0].astype(BF16), row(pool_scale[0]), wa, o, wb,
                row(xattn_norm_g[0]), w_mem_q[0].astype(BF16), row(mem_q_g[0]), mem_k, mem_v,
                w_mem_o[0].astype(BF16))
    x = _mlp(x, row(ff_norm_g[0]), w_ff1[0].astype(BF16), w_ff2[0].astype(BF16))

    qg2 = jnp.tile(na_q_g[0], LANES // NA_HEAD_DIM).reshape(1, LANES)
    kg2 = jnp.tile(na_k_g[0], LANES // NA_HEAD_DIM).reshape(1, LANES)
    q, k, v = _l1_front(x, row(mix_norm_g[1]), w_qkv_o[0].astype(BF16), qg2, kg2)
    c = _na_attn(q, k, v, _na_rpb_pairs(na_rpb[0]))
    x = _l1_out(x, c, w_out_o[0].astype(BF16), row(xattn_norm_g[1]), w_mem_q[1].astype(BF16),
                row(mem_q_g[1]), mem_k, mem_v, w_mem_o[1].astype(BF16))
    x = _mlp(x, row(ff_norm_g[1]), w_ff1[1].astype(BF16), w_ff2[1].astype(BF16))
    return x
```

```python
import functools

import jax
import jax.numpy as jnp
from jax import lax
from jax.experimental import pallas as pl
from jax.experimental.pallas import tpu as pltpu

F32 = jnp.float32
BF16 = jnp.bfloat16

D_MODEL = 1024
GRID_W = 64
EPS = 1e-6
POOL_WIDTH = 512
POOL_GROUP_W = 128
POOL_WINDOWS = (2, 4, 8, 16)
POOL_HALO = 8
MLA_HEADS = 8
MLA_NOPE = 64
MLA_ROPE = 32
MLA_V = 64
MLA_QK = MLA_NOPE + MLA_ROPE
Q_LORA = 256
KV_LORA = 128
ROPE_THETA = 10000.0
NA_HEADS = 16
NA_HEAD_DIM = 64
NA_KH = 8
NA_KW = 16
MEM_HEADS = 4
MEM_HEAD_DIM = 256
D_FF = 4096

LANES = 128
SUBLANES = 8
MASK_VALUE = -1e30
LOG2_E = 1.4426950408889634
VMEM_LIMIT = 48 * 1024 * 1024

TOKEN_TILE = 512
OUT_SUB_TILES = 2
MLA_Q_TILE = 512
MLA_KV_TILE = 512
MLA_Q_TILES_PER_STEP = 2
MLA_VT_ROWS = 80
NA_ROWS_PER_STEP = 16
NA_ROWS_AHEAD = 4


def _const_spec(shape):
    zeros = (0,) * len(shape)
    return pl.BlockSpec(shape, lambda *_: zeros, pipeline_mode=pl.Buffered(1))


def _params(**flags):
    return pltpu.CompilerParams(vmem_limit_bytes=VMEM_LIMIT, flags=flags or None)


def _rms(x, g, n=None):
    n = x.shape[-1] if n is None else n
    ms = jnp.sum(x * x, axis=-1, keepdims=True) * (1.0 / n)
    return x * lax.rsqrt(ms + EPS) * g


def _dot(a, b):
    return jnp.dot(a, b, preferred_element_type=F32)


def _dot_nt(a, b):
    return lax.dot_general(a, b, (((1,), (1,)), ((), ())), preferred_element_type=F32)


def _mem_kv_kernel(mem_ref, g_ref, w_ref, kg_ref, k_out, v_out):
    mn = _rms(mem_ref[0], g_ref[...]).astype(BF16)
    kv = _dot(mn, w_ref[...])
    for h in range(MEM_HEADS):
        sl = slice(h * MEM_HEAD_DIM, (h + 1) * MEM_HEAD_DIM)
        k_out[0, :, sl] = _rms(kv[:, sl], kg_ref[...]).astype(BF16)
    v_out[0] = kv[:, D_MODEL:].astype(BF16)


def _mem_kv(mem, g, w_kv, k_g):
    B, N, D = mem.shape
    return pl.pallas_call(
        _mem_kv_kernel,
        grid=(B,),
        in_specs=[
            pl.BlockSpec((1, N, D), lambda b: (b, 0, 0)),
            _const_spec((1, D)),
            _const_spec((D, 2 * D)),
            _const_spec((1, MEM_HEAD_DIM)),
        ],
        out_specs=[pl.BlockSpec((1, N, D), lambda b: (b, 0, 0))] * 2,
        out_shape=[jax.ShapeDtypeStruct((B, N, D), BF16)] * 2,
        compiler_params=_params(),
        name="mem_kv",
    )(mem, g, w_kv, k_g)


def _l0_front_kernel(x_ref, g_ref, win_ref, qlg_ref, wuqt_ref, kvlg_ref, wuk_ref, wuvt_ref,
                     qg_ref, kg_ref, cos_ref, sa_ref, sb_ref, cost_ref, sint_ref,
                     up_ref, q_ref, k_ref, v_ref):
    tm = x_ref.shape[1]
    h = _rms(x_ref[0], g_ref[...]).astype(BF16)
    u = _dot(h, win_ref[...])
    up_ref[0] = u[:, :POOL_WIDTH]
    o2 = POOL_WIDTH + Q_LORA
    o3 = o2 + KV_LORA
    cq = _rms(u[:, POOL_WIDTH:o2], qlg_ref[...])
    ckv = _rms(u[:, o2:o3], kvlg_ref[...])
    kr = u[:, o3:]
    qt = _dot(wuqt_ref[...], cq.T.astype(BF16))
    vt = _dot(wuvt_ref[...], ckv.T.astype(BF16))
    kf = _dot(ckv.astype(BF16), wuk_ref[...])

    half = MLA_ROPE // 2
    r1 = slice(MLA_NOPE, MLA_NOPE + half)
    r2 = slice(MLA_NOPE + half, MLA_QK)
    cos_t, sin_t = cost_ref[...], sint_ref[...]
    gq = jnp.tile(qg_ref[...], (1, tm // LANES))
    row_id = lax.broadcasted_iota(jnp.int32, (LANES, tm), 0)
    inv_n = 1.0 / MLA_QK
    for hd in range(MLA_HEADS):
        rows = slice(hd * LANES, (hd + 1) * LANES)
        blk = qt[rows]
        ms = jnp.sum(blk * blk, axis=0, keepdims=True) * inv_n
        y = blk * lax.rsqrt(ms + EPS) * gq
        y1, y2 = y[r1], y[r2]
        q_ref[0, hd] = jnp.concatenate(
            [y[:MLA_NOPE], y1 * cos_t - y2 * sin_t, y1 * sin_t + y2 * cos_t, y[MLA_QK:]], axis=0).astype(BF16)
        v_ref[0, hd] = jnp.where(row_id == MLA_V, 1.0, vt[rows]).astype(BF16)

    krg = kr * kg_ref[...]
    partner = pltpu.roll(krg, LANES - half, 1) * sa_ref[...] + pltpu.roll(krg, half, 1) * sb_ref[...]
    gcos = kg_ref[...] * cos_ref[...]
    for hd in range(MLA_HEADS):
        pre = kf[:, hd * LANES:(hd + 1) * LANES] + kr
        ms = jnp.sum(pre * pre, axis=-1, keepdims=True) * inv_n
        k_ref[0, hd] = ((pre * gcos + partner) * lax.rsqrt(ms + EPS)).astype(BF16)


def _l0_front(x, g, w_in, qlg, wuqt, kvlg, wuk, wuvt, qg, kg, cos, sa, sb, cos_t, sin_t):
    B, S, D = x.shape
    tm = TOKEN_TILE
    hw = MLA_HEADS * LANES
    head_spec = pl.BlockSpec((1, MLA_HEADS, tm, LANES), lambda b, i: (b, 0, i, 0))
    head_t_spec = pl.BlockSpec((1, MLA_HEADS, LANES, tm), lambda b, i: (b, 0, 0, i))
    tab_spec = pl.BlockSpec((tm, LANES), lambda b, i: (i, 0))
    tab_t_spec = pl.BlockSpec((MLA_ROPE // 2, tm), lambda b, i: (0, i))
    return pl.pallas_call(
        _l0_front_kernel,
        grid=(B, S // tm),
        in_specs=[
            pl.BlockSpec((1, tm, D), lambda b, i: (b, i, 0)),
            _const_spec((1, D)),
            _const_spec((D, D)),
            _const_spec((1, Q_LORA)),
            _const_spec((hw, Q_LORA)),
            _const_spec((1, KV_LORA)),
            _const_spec((KV_LORA, hw)),
            _const_spec((hw, KV_LORA)),
            _const_spec((LANES, LANES)),
            _const_spec((1, LANES)),
            tab_spec, tab_spec, tab_spec, tab_t_spec, tab_t_spec,
        ],
        out_specs=[
            pl.BlockSpec((1, tm, POOL_WIDTH), lambda b, i: (b, i, 0)),
            head_t_spec, head_spec, head_t_spec,
        ],
        out_shape=[
            jax.ShapeDtypeStruct((B, S, POOL_WIDTH), F32),
            jax.ShapeDtypeStruct((B, MLA_HEADS, LANES, S), BF16),
            jax.ShapeDtypeStruct((B, MLA_HEADS, S, LANES), BF16),
            jax.ShapeDtypeStruct((B, MLA_HEADS, LANES, S), BF16),
        ],
        compiler_params=_params(),
        name="l0_front",
    )(x, g, w_in, qlg, wuqt, kvlg, wuk, wuvt, qg, kg, cos, sa, sb, cos_t, sin_t)


def _mla_attn_kernel(qt_ref, k_ref, vt_ref, o_ref, s0_ref, s1_ref, s2_ref, acc0_ref, acc1_ref, *, tq, tk):
    nk = k_ref.shape[2] // tk
    nq = qt_ref.shape[3] // tq
    bufs = (s0_ref, s1_ref, s2_ref)
    accs = (acc0_ref, acc1_ref)
    chunk_max = {}
    z_st = pl.multiple_of(jnp.minimum(pl.program_id(2), 0) * SUBLANES, SUBLANES)
    z_ld = pl.multiple_of(jnp.minimum(pl.program_id(1), 0) * SUBLANES, SUBLANES)
    st_rows, ld_rows = pl.ds(z_st, tk), pl.ds(z_ld, tk)
    acc_st, acc_ld = pl.ds(z_st, MLA_VT_ROWS), pl.ds(z_ld, MLA_VT_ROWS)

    def fill(v):
        t, c = divmod(v, nk)
        qt = qt_ref[0, 0, :, t * tq:(t + 1) * tq]
        s = _dot(k_ref[0, 0, c * tk:(c + 1) * tk, :], qt)
        bufs[v % 3][st_rows, :] = s
        chunk_max[v] = jnp.max(s, axis=0, keepdims=True)

    def update(v, m_old):
        t, c = divmod(v, nk)
        s_ref, acc_ref = bufs[v % 3], accs[t % 2]
        vt = vt_ref[0, 0, :MLA_VT_ROWS, c * tk:(c + 1) * tk]
        cm = chunk_max.pop(v)
        m_new = cm if c == 0 else jnp.maximum(m_old, cm)
        p = jnp.exp2((s_ref[ld_rows, :] - m_new).astype(BF16))
        pv = _dot(vt, p)
        if c == 0:
            acc_ref[acc_st, :] = pv
        else:
            acc_ref[acc_st, :] = acc_ref[acc_ld, :] * jnp.exp2(m_old - m_new) + pv
        if c == nk - 1:
            acc = acc_ref[acc_ld, :]
            out = acc[:MLA_V] * (1.0 / acc[MLA_V:MLA_V + 1])
            out = jnp.concatenate([out, jnp.zeros((LANES - MLA_V, tq), F32)], axis=0)
            o_ref[0, t * tq:(t + 1) * tq, :] = out.T.astype(BF16)
        return m_new

    fill(0)
    fill(1)
    m = None
    for v in range(nq * nk):
        if v + 2 < nq * nk:
            fill(v + 2)
        m = update(v, m)


def _mla_attn(qt, k, vt):
    B, H, S, L = k.shape
    tq, tk, nq = MLA_Q_TILE, MLA_KV_TILE, MLA_Q_TILES_PER_STEP
    assert S % tk == 0 and S % (nq * tq) == 0
    return pl.pallas_call(
        functools.partial(_mla_attn_kernel, tq=tq, tk=tk),
        grid=(B, H, S // (nq * tq)),
        in_specs=[
            pl.BlockSpec((1, 1, L, nq * tq), lambda b, h, i: (b, h, 0, i)),
            pl.BlockSpec((1, 1, S, L), lambda b, h, i: (b, h, 0, 0)),
            pl.BlockSpec((1, 1, L, S), lambda b, h, i: (b, h, 0, 0)),
        ],
        out_specs=pl.BlockSpec((1, nq * tq, L), lambda b, h, i: (b, i, h)),
        out_shape=jax.ShapeDtypeStruct((B, S, H * L), BF16),
        scratch_shapes=[pltpu.VMEM((tk + SUBLANES, tq), F32)] * 3
        + [pltpu.VMEM((MLA_VT_ROWS + SUBLANES, tq), F32)] * 2,
        compiler_params=_params(),
        name="mla_attn",
    )(qt, k, vt)


def _pool_mix(up_ref, upp_ref, upn_ref, pw_ref, ps_ref, ext_ref, tm, seq_len):
    i = pl.program_id(1)
    last = pl.num_programs(1) - 1
    ext_ref[0:POOL_HALO] = jnp.where(i > 0, upp_ref[0], 0.0)
    ext_ref[POOL_HALO:POOL_HALO + tm] = up_ref[0]
    ext_ref[POOL_HALO + tm:2 * POOL_HALO + tm] = jnp.where(i < last, upn_ref[0], 0.0)
    t = i * tm + lax.broadcasted_iota(jnp.int32, (tm, 1), 0)
    outs = []
    for g, w in enumerate(POOL_WINDOWS):
        cols = slice(g * POOL_GROUP_W, (g + 1) * POOL_GROUP_W)
        wsum = None
        for o in range(-(w // 2), w - w // 2):
            piece = ext_ref[POOL_HALO + o:POOL_HALO + o + tm, cols]
            wsum = piece if wsum is None else wsum + piece
        lo = jnp.maximum(t - w // 2, 0)
        hi = jnp.minimum(t + (w - 1 - w // 2), seq_len - 1)
        cnt = (hi - lo + 1).astype(F32)
        d = wsum / cnt - ext_ref[POOL_HALO:POOL_HALO + tm, cols]
        outs.append(_dot(d.astype(BF16), pw_ref[g]))
    return jnp.concatenate(outs, axis=-1) * ps_ref[...]


def _mem_xattn(x1s, xg_ref, wq_ref, qg_ref, mk_ref, mv_ref, wo_ref):
    scale = MEM_HEAD_DIM ** -0.5
    heads = [slice(hd * MEM_HEAD_DIM, (hd + 1) * MEM_HEAD_DIM) for hd in range(MEM_HEADS)]
    qms = [_dot(_rms(x1, xg_ref[...]).astype(BF16), wq_ref[...]) for x1 in x1s]
    scores = [[_dot_nt((_rms(qm[:, sl], qg_ref[...]) * scale).astype(BF16), mk_ref[0, :, sl]) for sl in heads]
              for qm in qms]
    outs = []
    for x1, per_head in zip(x1s, scores):
        ohs = []
        for s, sl in zip(per_head, heads):
            e = jnp.exp(s - jnp.max(s, axis=-1, keepdims=True))
            denom = jnp.sum(e, axis=-1, keepdims=True)
            ohs.append((_dot(e.astype(BF16), mv_ref[0, :, sl]) * (1.0 / denom)).astype(BF16))
        outs.append(x1 + _dot(jnp.concatenate(ohs, axis=-1), wo_ref[...]))
    return outs


def _sub_tiles(tm):
    step = tm // OUT_SUB_TILES
    return [slice(r * step, (r + 1) * step) for r in range(OUT_SUB_TILES)]


def _l0_out_kernel(x_ref, up_ref, upp_ref, upn_ref, pw_ref, ps_ref, wa_ref, o_ref, wb_ref,
                   xg_ref, wq_ref, qg_ref, mk_ref, mv_ref, wo_ref, out_ref, ext_ref, *, tm, seq_len):
    a = _pool_mix(up_ref, upp_ref, upn_ref, pw_ref, ps_ref, ext_ref, tm, seq_len).astype(BF16)
    subs = _sub_tiles(tm)
    x1s = [x_ref[0, rows] + _dot(a[rows], wa_ref[...]) + _dot(o_ref[0, rows], wb_ref[...]) for rows in subs]
    for rows, out in zip(subs, _mem_xattn(x1s, xg_ref, wq_ref, qg_ref, mk_ref, mv_ref, wo_ref)):
        out_ref[0, rows] = out


def _l1_out_kernel(x_ref, o_ref, wb_ref, xg_ref, wq_ref, qg_ref, mk_ref, mv_ref, wo_ref, out_ref, *, tm):
    subs = _sub_tiles(tm)
    x1s = [x_ref[0, rows] + _dot(o_ref[0, rows], wb_ref[...]) for rows in subs]
    for rows, out in zip(subs, _mem_xattn(x1s, xg_ref, wq_ref, qg_ref, mk_ref, mv_ref, wo_ref)):
        out_ref[0, rows] = out


def _xattn_specs(n_mem):
    D = D_MODEL
    return [
        _const_spec((1, D)),
        _const_spec((D, D)),
        _const_spec((1, MEM_HEAD_DIM)),
        pl.BlockSpec((1, n_mem, D), lambda b, i: (b, 0, 0)),
        pl.BlockSpec((1, n_mem, D), lambda b, i: (b, 0, 0)),
        _const_spec((D, D)),
    ]


def _l0_out(x, up, pool_w, pool_scale, wa, o, wb, xg, wq, qg, mk, mv, wo):
    B, S, D = x.shape
    tm = TOKEN_TILE
    halo_blocks = tm // POOL_HALO
    tile = pl.BlockSpec((1, tm, D), lambda b, i: (b, i, 0))
    return pl.pallas_call(
        functools.partial(_l0_out_kernel, tm=tm, seq_len=S),
        grid=(B, S // tm),
        in_specs=[
            tile,
            pl.BlockSpec((1, tm, POOL_WIDTH), lambda b, i: (b, i, 0)),
            pl.BlockSpec((1, POOL_HALO, POOL_WIDTH),
                         lambda b, i: (b, jnp.maximum(i * halo_blocks - 1, 0), 0)),
            pl.BlockSpec((1, POOL_HALO, POOL_WIDTH),
                         lambda b, i: (b, jnp.minimum((i + 1) * halo_blocks, S // POOL_HALO - 1), 0)),
            _const_spec(pool_w.shape),
            _const_spec((1, POOL_WIDTH)),
            _const_spec((POOL_WIDTH, D)),
            pl.BlockSpec((1, tm, o.shape[-1]), lambda b, i: (b, i, 0)),
            _const_spec(wb.shape),
        ] + _xattn_specs(mk.shape[1]),
        out_specs=tile,
        out_shape=jax.ShapeDtypeStruct((B, S, D), F32),
        scratch_shapes=[pltpu.VMEM((tm + 2 * POOL_HALO, POOL_WIDTH), F32)],
        compiler_params=_params(),
        name="l0_out_xattn",
    )(x, up, up, up, pool_w, pool_scale, wa, o, wb, xg, wq, qg, mk, mv, wo)


def _l1_out(x, o, wb, xg, wq, qg, mk, mv, wo):
    B, S, D = x.shape
    tm = TOKEN_TILE
    tile = pl.BlockSpec((1, tm, D), lambda b, i: (b, i, 0))
    return pl.pallas_call(
        functools.partial(_l1_out_kernel, tm=tm),
        grid=(B, S // tm),
        in_specs=[tile, pl.BlockSpec((1, tm, o.shape[-1]), lambda b, i: (b, i, 0)), _const_spec(wb.shape)]
        + _xattn_specs(mk.shape[1]),
        out_specs=tile,
        out_shape=jax.ShapeDtypeStruct((B, S, D), F32),
        compiler_params=_params(),
        name="l1_out_xattn",
    )(x, o, wb, xg, wq, qg, mk, mv, wo)


def _mlp_kernel(x_ref, g_ref, w1_ref, w2_ref, out_ref, *, ff_chunk):
    x = x_ref[0]
    h = _rms(x, g_ref[...]).astype(BF16)
    acc = x
    for c in range(D_FF // ff_chunk):
        sl = slice(c * ff_chunk, (c + 1) * ff_chunk)
        a = jnp.maximum(_dot(h, w1_ref[:, sl]), 0.0)
        acc = acc + _dot((a * a).astype(BF16), w2_ref[sl, :])
    out_ref[0] = acc


def _mlp(x, g, w1, w2):
    B, S, D = x.shape
    tm = TOKEN_TILE
    tile = pl.BlockSpec((1, tm, D), lambda b, i: (b, i, 0))
    return pl.pallas_call(
        functools.partial(_mlp_kernel, ff_chunk=1024),
        grid=(B, S // tm),
        in_specs=[tile, _const_spec((1, D)), _const_spec((D, D_FF)), _const_spec((D_FF, D))],
        out_specs=tile,
        out_shape=jax.ShapeDtypeStruct((B, S, D), F32),
        compiler_params=_params(),
        name="mlp",
    )(x, g, w1, w2)


def _pair_norm(z, g2, lo_mask):
    y = z * z
    lo = jnp.sum(jnp.where(lo_mask, y, 0.0), axis=-1, keepdims=True)
    hi = jnp.sum(jnp.where(lo_mask, 0.0, y), axis=-1, keepdims=True)
    inv = 1.0 / NA_HEAD_DIM
    r = jnp.where(lo_mask, lax.rsqrt(lo * inv + EPS), lax.rsqrt(hi * inv + EPS))
    return z * r * g2


def _l1_front_kernel(x_ref, g_ref, w_ref, qg_ref, kg_ref, q_out, k_out, v_out):
    h = _rms(x_ref[0], g_ref[...]).astype(BF16)
    qkv = _dot(h, w_ref[...])
    width = NA_HEADS * NA_HEAD_DIM
    lo_mask = lax.broadcasted_iota(jnp.int32, (1, LANES), 1) < NA_HEAD_DIM
    scale = NA_HEAD_DIM ** -0.5
    for j in range(width // LANES):
        sl = slice(j * LANES, (j + 1) * LANES)
        q_out[0, :, sl] = (_pair_norm(qkv[:, sl], qg_ref[...], lo_mask) * scale).astype(BF16)
        ksl = slice(width + j * LANES, width + (j + 1) * LANES)
        k_out[0, :, sl] = _pair_norm(qkv[:, ksl], kg_ref[...], lo_mask).astype(BF16)
    v_out[0] = qkv[:, 2 * width:].astype(BF16)


def _l1_front(x, g, w_qkv, qg2, kg2):
    B, S, D = x.shape
    tm = TOKEN_TILE
    tile = pl.BlockSpec((1, tm, D), lambda b, i: (b, i, 0))
    return pl.pallas_call(
        _l1_front_kernel,
        grid=(B, S // tm),
        in_specs=[tile, _const_spec((1, D)), _const_spec((D, 3 * D)), _const_spec((1, LANES)),
                  _const_spec((1, LANES))],
        out_specs=[tile] * 3,
        out_shape=[jax.ShapeDtypeStruct((B, S, D), BF16)] * 3,
        compiler_params=_params(),
        name="l1_front",
    )(x, g, w_qkv, qg2, kg2)


def _na_bias_tiles(rpb_ref, tab_ref):
    c = lax.broadcasted_iota(jnp.int32, (GRID_W, LANES), 0)
    kc = lax.broadcasted_iota(jnp.int32, (GRID_W, LANES), 1) % GRID_W
    c0 = jnp.clip(c - NA_KW // 2, 0, GRID_W - NA_KW)
    valid = (kc >= c0) & (kc < c0 + NA_KW)
    for h in range(tab_ref.shape[0]):
        for dr in range(tab_ref.shape[1]):
            rows = jnp.broadcast_to(rpb_ref[h, dr:dr + 1, :], (GRID_W, LANES))
            skew = pltpu.roll(rows, LANES - (NA_KW - 1), 1, stride=1, stride_axis=0)
            tab_ref[h, dr] = jnp.where(valid, skew, MASK_VALUE)


def _na_kernel(q_ref, k_ref, v_ref, rpb_ref, o_ref, tab_ref, *s_refs, rows_per_step, n_rows):
    i = pl.program_id(2)
    win = NA_KH * GRID_W
    lo_mask = lax.broadcasted_iota(jnp.int32, (1, LANES), 1) < NA_HEAD_DIM
    zero = jnp.zeros((), BF16)

    @pl.when(i == 0)
    def _():
        _na_bias_tiles(rpb_ref, tab_ref)

    z_st = pl.multiple_of(jnp.minimum(pl.program_id(2), 0) * SUBLANES, SUBLANES)
    z_ld = pl.multiple_of(jnp.minimum(pl.program_id(1), 0) * SUBLANES, SUBLANES)
    st_rows, ld_rows = pl.ds(z_st, 2 * GRID_W), pl.ds(z_ld, 2 * GRID_W)

    def window(rr):
        r = i * rows_per_step + rr
        r0 = jnp.clip(r - NA_KH // 2, 0, n_rows - NA_KH)
        return r - r0, pl.ds(pl.multiple_of(r0 * GRID_W, GRID_W), win)

    def scores(rr):
        delta, keys = window(rr)
        q_row = q_ref[0, rr * GRID_W:(rr + 1) * GRID_W, :]
        q2 = jnp.concatenate([jnp.where(lo_mask, q_row, zero), jnp.where(lo_mask, zero, q_row)], axis=0)
        bias = jnp.concatenate(
            [jnp.concatenate([tab_ref[h, 2 * p + (NA_KH - 1) - delta] for p in range(NA_KH // 2)], axis=1)
             for h in range(2)], axis=0)
        s_refs[rr][st_rows, :] = _dot_nt(q2, k_ref[0, keys, :]) + bias

    def attend(rr):
        _, keys = window(rr)
        s_ref = s_refs[rr]
        m = jnp.max(s_ref[ld_rows, :], axis=-1, keepdims=True)
        e = jnp.exp(s_ref[ld_rows, :] - m)
        denom = jnp.sum(e, axis=-1, keepdims=True)
        o2 = _dot(e.astype(BF16), v_ref[0, keys, :]) * (1.0 / denom)
        out = jnp.where(lo_mask, o2[:GRID_W], o2[GRID_W:])
        o_ref[0, rr * GRID_W:(rr + 1) * GRID_W, :] = out.astype(BF16)

    for rr in range(min(NA_ROWS_AHEAD, rows_per_step)):
        scores(rr)
    for rr in range(rows_per_step):
        if rr + NA_ROWS_AHEAD < rows_per_step:
            scores(rr + NA_ROWS_AHEAD)
        attend(rr)


def _na_attn(q, k, v, rpb_pairs):
    B, S, D = q.shape
    n_rows = S // GRID_W
    tq = NA_ROWS_PER_STEP * GRID_W
    n_dr = rpb_pairs.shape[1]
    kv_spec = pl.BlockSpec((1, S, LANES), lambda b, j, i: (b, 0, j))
    return pl.pallas_call(
        functools.partial(_na_kernel, rows_per_step=NA_ROWS_PER_STEP, n_rows=n_rows),
        grid=(B, D // LANES, S // tq),
        in_specs=[
            pl.BlockSpec((1, tq, LANES), lambda b, j, i: (b, i, j)),
            kv_spec, kv_spec,
            pl.BlockSpec((2, n_dr, LANES), lambda b, j, i: (j, 0, 0)),
        ],
        out_specs=pl.BlockSpec((1, tq, LANES), lambda b, j, i: (b, i, j)),
        out_shape=jax.ShapeDtypeStruct((B, S, D), BF16),
        scratch_shapes=[pltpu.VMEM((2, n_dr, GRID_W, LANES), F32)]
        + [pltpu.VMEM((2 * GRID_W + SUBLANES, NA_KH * GRID_W), F32)] * NA_ROWS_PER_STEP,
        compiler_params=_params(),
        name="na_attn",
    )(q, k, v, rpb_pairs)


def _na_rpb_pairs(rpb):
    padded = jnp.pad(rpb, ((0, 0), (0, 0), (0, GRID_W - rpb.shape[-1])))
    return jnp.concatenate([padded[:, :-1], padded[:, 1:]], axis=-1)


def _pad_heads(w, heads, width):
    lead = w.shape[:-1]
    w = w.reshape(*lead, heads, width)
    w = jnp.pad(w, [(0, 0)] * len(lead) + [(0, 0), (0, LANES - width)])
    return w.reshape(*lead, heads * LANES)


def _rope_tables(seq_len):
    half = MLA_ROPE // 2
    pos = jnp.arange(seq_len, dtype=F32)
    freqs = ROPE_THETA ** (-jnp.arange(half, dtype=F32) / half)
    ang = pos[:, None] * freqs[None, :]
    cos, sin = jnp.cos(ang), jnp.sin(ang)
    z = lambda n: jnp.zeros((seq_len, n), F32)
    cos_t = jnp.concatenate([jnp.ones((seq_len, MLA_NOPE), F32), cos, cos, z(LANES - MLA_QK)], axis=1)
    sin_a = jnp.concatenate([z(MLA_NOPE), -sin, z(half), z(LANES - MLA_QK)], axis=1)
    sin_b = jnp.concatenate([z(MLA_NOPE), z(half), sin, z(LANES - MLA_QK)], axis=1)
    return cos_t, sin_a, sin_b, cos.T, sin.T


def kernel(x, mem, mix_norm_g, xattn_norm_g, ff_norm_g, w_mem_q, mem_q_g, w_mem_o, w_ff1, w_ff2, mem_tok_norm_g, w_mem_kv, mem_k_g, w_in_e, pool_w, pool_scale, q_lora_g, w_uq, kv_lora_g, w_ukv, mla_q_g, mla_k_g, w_out_e, w_qkv_o, na_q_g, na_k_g, na_rpb, w_out_o):
    B, S, D = x.shape
    row = lambda v: v.reshape(1, -1)

    mem_k, mem_v = _mem_kv(mem, row(mem_tok_norm_g), w_mem_kv.astype(BF16), row(mem_k_g))

    o1 = POOL_WIDTH
    o3 = o1 + Q_LORA + KV_LORA
    w_in = w_in_e[0]
    w_in_pad = jnp.concatenate(
        [w_in[:, :o3], jnp.zeros((D, MLA_NOPE), F32), w_in[:, o3:], jnp.zeros((D, LANES - MLA_QK), F32)],
        axis=1).astype(BF16)
    wuqt = _pad_heads(w_uq[0], MLA_HEADS, MLA_QK).T.astype(BF16)
    w_ukv_h = w_ukv[0].reshape(KV_LORA, MLA_HEADS, MLA_NOPE + MLA_V)
    wuk = _pad_heads(w_ukv_h[:, :, :MLA_NOPE].reshape(KV_LORA, -1), MLA_HEADS, MLA_NOPE).astype(BF16)
    wuvt = _pad_heads(w_ukv_h[:, :, MLA_NOPE:].reshape(KV_LORA, -1), MLA_HEADS, MLA_V).T.astype(BF16)
    qg = jnp.pad(mla_q_g[0], (0, LANES - MLA_QK)) * (MLA_QK ** -0.5 * LOG2_E)
    qg = jnp.broadcast_to(qg[:, None], (LANES, LANES))
    kg = jnp.pad(mla_k_g[0], (0, LANES - MLA_QK)).reshape(1, LANES)
    up, q, k, v = _l0_front(x, row(mix_norm_g[0]), w_in_pad, row(q_lora_g[0]), wuqt, row(kv_lora_g[0]),
                            wuk, wuvt, qg, kg, *_rope_tables(S))
    o = _mla_attn(q, k, v)
    w_out = w_out_e[0]
    wa = w_out[:POOL_WIDTH].astype(BF16)
    wb = jnp.pad(w_out[POOL_WIDTH:].reshape(MLA_HEADS, MLA_V, D), ((0, 0), (0, LANES - MLA_V), (0, 0)))
    wb = wb.reshape(MLA_HEADS * LANES, D).astype(BF16)
    x = _l0_out(x, up, pool_w[0].astype(BF16), row(pool_scale[0]), wa, o, wb,
                row(xattn_norm_g[0]), w_mem_q[0].astype(BF16), row(mem_q_g[0]), mem_k, mem_v,
                w_mem_o[0].astype(BF16))
    x = _mlp(x, row(ff_norm_g[0]), w_ff1[0].astype(BF16), w_ff2[0].astype(BF16))

    qg2 = jnp.tile(na_q_g[0], LANES // NA_HEAD_DIM).reshape(1, LANES)
    kg2 = jnp.tile(na_k_g[0], LANES // NA_HEAD_DIM).reshape(1, LANES)
    q, k, v = _l1_front(x, row(mix_norm_g[1]), w_qkv_o[0].astype(BF16), qg2, kg2)
    c = _na_attn(q, k, v, _na_rpb_pairs(na_rpb[0]))
    x = _l1_out(x, c, w_out_o[0].astype(BF16), row(xattn_norm_g[1]), w_mem_q[1].astype(BF16),
                row(mem_q_g[1]), mem_k, mem_v, w_mem_o[1].astype(BF16))
    x = _mlp(x, row(ff_norm_g[1]), w_ff1[1].astype(BF16), w_ff2[1].astype(BF16))
    return x
```

```python
import functools

import jax
import jax.numpy as jnp
from jax import lax
from jax.experimental import pallas as pl
from jax.experimental.pallas import tpu as pltpu

F32 = jnp.float32
BF16 = jnp.bfloat16

D_MODEL = 1024
GRID_W = 64
EPS = 1e-6
POOL_WIDTH = 512
POOL_GROUP_W = 128
POOL_WINDOWS = (2, 4, 8, 16)
POOL_HALO = 8
MLA_HEADS = 8
MLA_NOPE = 64
MLA_ROPE = 32
MLA_V = 64
MLA_QK = MLA_NOPE + MLA_ROPE
Q_LORA = 256
KV_LORA = 128
ROPE_THETA = 10000.0
NA_HEADS = 16
NA_HEAD_DIM = 64
NA_KH = 8
NA_KW = 16
MEM_HEADS = 4
MEM_HEAD_DIM = 256
D_FF = 4096

LANES = 128
SUBLANES = 8
MASK_VALUE = -1e30
LOG2_E = 1.4426950408889634
VMEM_LIMIT = 56 * 1024 * 1024

TOKEN_TILE = 512
OUT_SUB_TILES = 2
FF_CHUNK = 512
MLA_Q_TILE = 512
MLA_KV_TILE = 512
MLA_Q_TILES_PER_STEP = 2
MLA_VT_ROWS = 80
NA_ROWS_PER_STEP = 32
NA_ROWS_AHEAD = 4


def _const_spec(shape):
    zeros = (0,) * len(shape)
    return pl.BlockSpec(shape, lambda *_: zeros, pipeline_mode=pl.Buffered(1))


def _params(**flags):
    return pltpu.CompilerParams(vmem_limit_bytes=VMEM_LIMIT, flags=flags or None)


def _rms(x, g, n=None):
    n = x.shape[-1] if n is None else n
    ms = jnp.sum(x * x, axis=-1, keepdims=True) * (1.0 / n)
    return x * lax.rsqrt(ms + EPS) * g


def _dot(a, b):
    return jnp.dot(a, b, preferred_element_type=F32)


def _dot_nt(a, b):
    return lax.dot_general(a, b, (((1,), (1,)), ((), ())), preferred_element_type=F32)


def _mem_kv_kernel(mem_ref, g_ref, w_ref, kg_ref, k_out, v_out):
    mn = _rms(mem_ref[0], g_ref[...]).astype(BF16)
    kv = _dot(mn, w_ref[...])
    for h in range(MEM_HEADS):
        sl = slice(h * MEM_HEAD_DIM, (h + 1) * MEM_HEAD_DIM)
        k_out[0, :, sl] = _rms(kv[:, sl], kg_ref[...]).astype(BF16)
    v_out[0] = kv[:, D_MODEL:].astype(BF16)


def _mem_kv(mem, g, w_kv, k_g):
    B, N, D = mem.shape
    return pl.pallas_call(
        _mem_kv_kernel,
        grid=(B,),
        in_specs=[
            pl.BlockSpec((1, N, D), lambda b: (b, 0, 0)),
            _const_spec((1, D)),
            _const_spec((D, 2 * D)),
            _const_spec((1, MEM_HEAD_DIM)),
        ],
        out_specs=[pl.BlockSpec((1, N, D), lambda b: (b, 0, 0))] * 2,
        out_shape=[jax.ShapeDtypeStruct((B, N, D), BF16)] * 2,
        compiler_params=_params(),
        name="mem_kv",
    )(mem, g, w_kv, k_g)


def _l0_front_kernel(x_ref, g_ref, win_ref, qlg_ref, wuqt_ref, kvlg_ref, wuk_ref, wuvt_ref,
                     qg_ref, kg_ref, cos_ref, sa_ref, sb_ref, cost_ref, sint_ref,
                     up_ref, q_ref, k_ref, v_ref):
    tm = x_ref.shape[1]
    h = _rms(x_ref[0], g_ref[...]).astype(BF16)
    u = _dot(h, win_ref[...])
    up_ref[0] = u[:, :POOL_WIDTH]
    o2 = POOL_WIDTH + Q_LORA
    o3 = o2 + KV_LORA
    cq = _rms(u[:, POOL_WIDTH:o2], qlg_ref[...])
    ckv = _rms(u[:, o2:o3], kvlg_ref[...])
    kr = u[:, o3:]
    qt = _dot(wuqt_ref[...], cq.T.astype(BF16))
    vt = _dot(wuvt_ref[...], ckv.T.astype(BF16))
    kf = _dot(ckv.astype(BF16), wuk_ref[...])

    half = MLA_ROPE // 2
    r1 = slice(MLA_NOPE, MLA_NOPE + half)
    r2 = slice(MLA_NOPE + half, MLA_QK)
    cos_t, sin_t = cost_ref[...], sint_ref[...]
    gq = jnp.tile(qg_ref[...], (1, tm // LANES))
    row_id = lax.broadcasted_iota(jnp.int32, (LANES, tm), 0)
    inv_n = 1.0 / MLA_QK
    for hd in range(MLA_HEADS):
        rows = slice(hd * LANES, (hd + 1) * LANES)
        blk = qt[rows]
        ms = jnp.sum(blk * blk, axis=0, keepdims=True) * inv_n
        y = blk * lax.rsqrt(ms + EPS) * gq
        y1, y2 = y[r1], y[r2]
        q_ref[0, hd] = jnp.concatenate(
            [y[:MLA_NOPE], y1 * cos_t - y2 * sin_t, y1 * sin_t + y2 * cos_t, y[MLA_QK:]], axis=0).astype(BF16)
        v_ref[0, hd] = jnp.where(row_id == MLA_V, 1.0, vt[rows]).astype(BF16)

    krg = kr * kg_ref[...]
    partner = pltpu.roll(krg, LANES - half, 1) * sa_ref[...] + pltpu.roll(krg, half, 1) * sb_ref[...]
    gcos = kg_ref[...] * cos_ref[...]
    for hd in range(MLA_HEADS):
        pre = kf[:, hd * LANES:(hd + 1) * LANES] + kr
        ms = jnp.sum(pre * pre, axis=-1, keepdims=True) * inv_n
        k_ref[0, hd] = ((pre * gcos + partner) * lax.rsqrt(ms + EPS)).astype(BF16)


def _l0_front(x, g, w_in, qlg, wuqt, kvlg, wuk, wuvt, qg, kg, cos, sa, sb, cos_t, sin_t):
    B, S, D = x.shape
    tm = TOKEN_TILE
    hw = MLA_HEADS * LANES
    head_spec = pl.BlockSpec((1, MLA_HEADS, tm, LANES), lambda b, i: (b, 0, i, 0))
    head_t_spec = pl.BlockSpec((1, MLA_HEADS, LANES, tm), lambda b, i: (b, 0, 0, i))
    tab_spec = pl.BlockSpec((tm, LANES), lambda b, i: (i, 0))
    tab_t_spec = pl.BlockSpec((MLA_ROPE // 2, tm), lambda b, i: (0, i))
    return pl.pallas_call(
        _l0_front_kernel,
        grid=(B, S // tm),
        in_specs=[
            pl.BlockSpec((1, tm, D), lambda b, i: (b, i, 0)),
            _const_spec((1, D)),
            _const_spec((D, D)),
            _const_spec((1, Q_LORA)),
            _const_spec((hw, Q_LORA)),
            _const_spec((1, KV_LORA)),
            _const_spec((KV_LORA, hw)),
            _const_spec((hw, KV_LORA)),
            _const_spec((LANES, LANES)),
            _const_spec((1, LANES)),
            tab_spec, tab_spec, tab_spec, tab_t_spec, tab_t_spec,
        ],
        out_specs=[
            pl.BlockSpec((1, tm, POOL_WIDTH), lambda b, i: (b, i, 0)),
            head_t_spec, head_spec, head_t_spec,
        ],
        out_shape=[
            jax.ShapeDtypeStruct((B, S, POOL_WIDTH), F32),
            jax.ShapeDtypeStruct((B, MLA_HEADS, LANES, S), BF16),
            jax.ShapeDtypeStruct((B, MLA_HEADS, S, LANES), BF16),
            jax.ShapeDtypeStruct((B, MLA_HEADS, LANES, S), BF16),
        ],
        compiler_params=_params(),
        name="l0_front",
    )(x, g, w_in, qlg, wuqt, kvlg, wuk, wuvt, qg, kg, cos, sa, sb, cos_t, sin_t)


def _mla_attn_kernel(qt_ref, k_ref, vt_ref, o_ref, s0_ref, s1_ref, s2_ref, acc0_ref, acc1_ref, *, tq, tk):
    nk = k_ref.shape[2] // tk
    nq = qt_ref.shape[3] // tq
    bufs = (s0_ref, s1_ref, s2_ref)
    accs = (acc0_ref, acc1_ref)
    chunk_max = {}
    z_st = pl.multiple_of(jnp.minimum(pl.program_id(2), 0) * SUBLANES, SUBLANES)
    z_ld = pl.multiple_of(jnp.minimum(pl.program_id(1), 0) * SUBLANES, SUBLANES)
    st_rows, ld_rows = pl.ds(z_st, tk), pl.ds(z_ld, tk)
    acc_st, acc_ld = pl.ds(z_st, MLA_VT_ROWS), pl.ds(z_ld, MLA_VT_ROWS)

    def fill(v):
        t, c = divmod(v, nk)
        qt = qt_ref[0, 0, :, t * tq:(t + 1) * tq]
        s = _dot(k_ref[0, 0, c * tk:(c + 1) * tk, :], qt)
        bufs[v % 3][st_rows, :] = s
        chunk_max[v] = jnp.max(s, axis=0, keepdims=True)

    def update(v, m_old):
        t, c = divmod(v, nk)
        s_ref, acc_ref = bufs[v % 3], accs[t % 2]
        vt = vt_ref[0, 0, :MLA_VT_ROWS, c * tk:(c + 1) * tk]
        cm = chunk_max.pop(v)
        m_new = cm if c == 0 else jnp.maximum(m_old, cm)
        p = jnp.exp2((s_ref[ld_rows, :] - m_new).astype(BF16))
        pv = _dot(vt, p)
        if c == 0:
            acc_ref[acc_st, :] = pv
        else:
            acc_ref[acc_st, :] = acc_ref[acc_ld, :] * jnp.exp2(m_old - m_new) + pv
        if c == nk - 1:
            acc = acc_ref[acc_ld, :]
            out = acc[:MLA_V] * (1.0 / acc[MLA_V:MLA_V + 1])
            out = jnp.concatenate([out, jnp.zeros((LANES - MLA_V, tq), F32)], axis=0)
            o_ref[0, t * tq:(t + 1) * tq, :] = out.T.astype(BF16)
        return m_new

    fill(0)
    fill(1)
    m = None
    for v in range(nq * nk):
        if v + 2 < nq * nk:
            fill(v + 2)
        m = update(v, m)


def _mla_attn(qt, k, vt):
    B, H, S, L = k.shape
    tq, tk, nq = MLA_Q_TILE, MLA_KV_TILE, MLA_Q_TILES_PER_STEP
    assert S % tk == 0 and S % (nq * tq) == 0
    return pl.pallas_call(
        functools.partial(_mla_attn_kernel, tq=tq, tk=tk),
        grid=(B, H, S // (nq * tq)),
        in_specs=[
            pl.BlockSpec((1, 1, L, nq * tq), lambda b, h, i: (b, h, 0, i)),
            pl.BlockSpec((1, 1, S, L), lambda b, h, i: (b, h, 0, 0)),
            pl.BlockSpec((1, 1, L, S), lambda b, h, i: (b, h, 0, 0)),
        ],
        out_specs=pl.BlockSpec((1, nq * tq, L), lambda b, h, i: (b, i, h)),
        out_shape=jax.ShapeDtypeStruct((B, S, H * L), BF16),
        scratch_shapes=[pltpu.VMEM((tk + SUBLANES, tq), F32)] * 3
        + [pltpu.VMEM((MLA_VT_ROWS + SUBLANES, tq), F32)] * 2,
        compiler_params=_params(),
        name="mla_attn",
    )(qt, k, vt)


def _pool_mix(up_ref, upp_ref, upn_ref, pw_ref, ps_ref, ext_ref, tm, seq_len):
    i = pl.program_id(1)
    last = pl.num_programs(1) - 1
    ext_ref[0:POOL_HALO] = jnp.where(i > 0, upp_ref[0], 0.0)
    ext_ref[POOL_HALO:POOL_HALO + tm] = up_ref[0]
    ext_ref[POOL_HALO + tm:2 * POOL_HALO + tm] = jnp.where(i < last, upn_ref[0], 0.0)
    t = i * tm + lax.broadcasted_iota(jnp.int32, (tm, 1), 0)
    outs = []
    for g, w in enumerate(POOL_WINDOWS):
        cols = slice(g * POOL_GROUP_W, (g + 1) * POOL_GROUP_W)
        wsum = None
        for o in range(-(w // 2), w - w // 2):
            piece = ext_ref[POOL_HALO + o:POOL_HALO + o + tm, cols]
            wsum = piece if wsum is None else wsum + piece
        lo = jnp.maximum(t - w // 2, 0)
        hi = jnp.minimum(t + (w - 1 - w // 2), seq_len - 1)
        cnt = (hi - lo + 1).astype(F32)
        d = wsum / cnt - ext_ref[POOL_HALO:POOL_HALO + tm, cols]
        outs.append(_dot(d.astype(BF16), pw_ref[g]))
    return jnp.concatenate(outs, axis=-1) * ps_ref[...]


def _mem_xattn(x1s, xg_ref, wq_ref, qg_ref, mk_ref, mv_ref, wo_ref):
    scale = MEM_HEAD_DIM ** -0.5
    heads = [slice(hd * MEM_HEAD_DIM, (hd + 1) * MEM_HEAD_DIM) for hd in range(MEM_HEADS)]
    qms = [_dot(_rms(x1, xg_ref[...]).astype(BF16), wq_ref[...]) for x1 in x1s]
    scores = [[_dot_nt((_rms(qm[:, sl], qg_ref[...]) * scale).astype(BF16), mk_ref[0, :, sl]) for sl in heads]
              for qm in qms]
    outs = []
    for x1, per_head in zip(x1s, scores):
        ohs = []
        for s, sl in zip(per_head, heads):
            e = jnp.exp(s - jnp.max(s, axis=-1, keepdims=True))
            denom = jnp.sum(e, axis=-1, keepdims=True)
            ohs.append((_dot(e.astype(BF16), mv_ref[0, :, sl]) * (1.0 / denom)).astype(BF16))
        outs.append(x1 + _dot(jnp.concatenate(ohs, axis=-1), wo_ref[...]))
    return outs


def _sub_tiles(tm):
    step = tm // OUT_SUB_TILES
    return [slice(r * step, (r + 1) * step) for r in range(OUT_SUB_TILES)]


def _mlp(x2s, fg_ref, w1_ref, w2_ref):
    outs = []
    for x2 in x2s:
        h = _rms(x2, fg_ref[...]).astype(BF16)
        acc = x2
        for c in range(D_FF // FF_CHUNK):
            sl = slice(c * FF_CHUNK, (c + 1) * FF_CHUNK)
            a = jnp.maximum(_dot(h, w1_ref[:, sl]), 0.0)
            acc = acc + _dot((a * a).astype(BF16), w2_ref[sl, :])
        outs.append(acc)
    return outs


def _layer_tail(x1s, tail_refs, out_ref):
    xg_ref, wq_ref, qg_ref, mk_ref, mv_ref, wo_ref, fg_ref, w1_ref, w2_ref = tail_refs
    x2s = _mem_xattn(x1s, xg_ref, wq_ref, qg_ref, mk_ref, mv_ref, wo_ref)
    out_ref[0] = _mlp([jnp.concatenate(x2s, axis=0)], fg_ref, w1_ref, w2_ref)[0]


def _l0_out_kernel(x_ref, up_ref, upp_ref, upn_ref, pw_ref, ps_ref, wa_ref, o_ref, wb_ref, *rest, tm, seq_len):
    *tail_refs, out_ref, ext_ref = rest
    a = _pool_mix(up_ref, upp_ref, upn_ref, pw_ref, ps_ref, ext_ref, tm, seq_len).astype(BF16)
    subs = _sub_tiles(tm)
    x1s = [x_ref[0, rows] + _dot(a[rows], wa_ref[...]) + _dot(o_ref[0, rows], wb_ref[...]) for rows in subs]
    _layer_tail(x1s, tail_refs, out_ref)


def _l1_out_kernel(x_ref, o_ref, wb_ref, *rest, tm):
    *tail_refs, out_ref = rest
    subs = _sub_tiles(tm)
    x1s = [x_ref[0, rows] + _dot(o_ref[0, rows], wb_ref[...]) for rows in subs]
    _layer_tail(x1s, tail_refs, out_ref)


def _tail_specs(n_mem):
    D = D_MODEL
    return [
        _const_spec((1, D)),
        _const_spec((D, D)),
        _const_spec((1, MEM_HEAD_DIM)),
        pl.BlockSpec((1, n_mem, D), lambda b, i: (b, 0, 0), pipeline_mode=pl.Buffered(1)),
        pl.BlockSpec((1, n_mem, D), lambda b, i: (b, 0, 0), pipeline_mode=pl.Buffered(1)),
        _const_spec((D, D)),
        _const_spec((1, D)),
        _const_spec((D, D_FF)),
        _const_spec((D_FF, D)),
    ]


def _l0_out(x, up, pool_w, pool_scale, wa, o, wb, tail):
    B, S, D = x.shape
    n_mem = tail[3].shape[1]
    tm = TOKEN_TILE
    halo_blocks = tm // POOL_HALO
    tile = pl.BlockSpec((1, tm, D), lambda b, i: (b, i, 0))
    return pl.pallas_call(
        functools.partial(_l0_out_kernel, tm=tm, seq_len=S),
        grid=(B, S // tm),
        in_specs=[
            tile,
            pl.BlockSpec((1, tm, POOL_WIDTH), lambda b, i: (b, i, 0)),
            pl.BlockSpec((1, POOL_HALO, POOL_WIDTH),
                         lambda b, i: (b, jnp.maximum(i * halo_blocks - 1, 0), 0)),
            pl.BlockSpec((1, POOL_HALO, POOL_WIDTH),
                         lambda b, i: (b, jnp.minimum((i + 1) * halo_blocks, S // POOL_HALO - 1), 0)),
            _const_spec(pool_w.shape),
            _const_spec((1, POOL_WIDTH)),
            _const_spec((POOL_WIDTH, D)),
            pl.BlockSpec((1, tm, o.shape[-1]), lambda b, i: (b, i, 0)),
            _const_spec(wb.shape),
        ] + _tail_specs(n_mem),
        out_specs=tile,
        out_shape=jax.ShapeDtypeStruct((B, S, D), F32),
        scratch_shapes=[pltpu.VMEM((tm + 2 * POOL_HALO, POOL_WIDTH), F32)],
        compiler_params=_params(),
        name="l0_tail",
    )(x, up, up, up, pool_w, pool_scale, wa, o, wb, *tail)


def _l1_out(x, o, wb, tail):
    B, S, D = x.shape
    n_mem = tail[3].shape[1]
    tm = TOKEN_TILE
    tile = pl.BlockSpec((1, tm, D), lambda b, i: (b, i, 0))
    return pl.pallas_call(
        functools.partial(_l1_out_kernel, tm=tm),
        grid=(B, S // tm),
        in_specs=[tile, pl.BlockSpec((1, tm, o.shape[-1]), lambda b, i: (b, i, 0)), _const_spec(wb.shape)]
        + _tail_specs(n_mem),
        out_specs=tile,
        out_shape=jax.ShapeDtypeStruct((B, S, D), F32),
        compiler_params=_params(),
        name="l1_tail",
    )(x, o, wb, *tail)


def _pair_norm(z, g2, lo_mask):
    y = z * z
    lo = jnp.sum(jnp.where(lo_mask, y, 0.0), axis=-1, keepdims=True)
    hi = jnp.sum(jnp.where(lo_mask, 0.0, y), axis=-1, keepdims=True)
    inv = 1.0 / NA_HEAD_DIM
    r = jnp.where(lo_mask, lax.rsqrt(lo * inv + EPS), lax.rsqrt(hi * inv + EPS))
    return z * r * g2


def _l1_front_kernel(x_ref, g_ref, w_ref, qg_ref, kg_ref, q_out, k_out, v_out):
    h = _rms(x_ref[0], g_ref[...]).astype(BF16)
    qkv = _dot(h, w_ref[...])
    width = NA_HEADS * NA_HEAD_DIM
    lo_mask = lax.broadcasted_iota(jnp.int32, (1, LANES), 1) < NA_HEAD_DIM
    scale = NA_HEAD_DIM ** -0.5
    for j in range(width // LANES):
        sl = slice(j * LANES, (j + 1) * LANES)
        q_out[0, :, sl] = (_pair_norm(qkv[:, sl], qg_ref[...], lo_mask) * scale).astype(BF16)
        ksl = slice(width + j * LANES, width + (j + 1) * LANES)
        k_out[0, :, sl] = _pair_norm(qkv[:, ksl], kg_ref[...], lo_mask).astype(BF16)
    v_out[0] = qkv[:, 2 * width:].astype(BF16)


def _l1_front(x, g, w_qkv, qg2, kg2):
    B, S, D = x.shape
    tm = TOKEN_TILE
    tile = pl.BlockSpec((1, tm, D), lambda b, i: (b, i, 0))
    return pl.pallas_call(
        _l1_front_kernel,
        grid=(B, S // tm),
        in_specs=[tile, _const_spec((1, D)), _const_spec((D, 3 * D)), _const_spec((1, LANES)),
                  _const_spec((1, LANES))],
        out_specs=[tile] * 3,
        out_shape=[jax.ShapeDtypeStruct((B, S, D), BF16)] * 3,
        compiler_params=_params(),
        name="l1_front",
    )(x, g, w_qkv, qg2, kg2)


def _na_bias_tiles(rpb_ref, tab_ref):
    c = lax.broadcasted_iota(jnp.int32, (GRID_W, LANES), 0)
    kc = lax.broadcasted_iota(jnp.int32, (GRID_W, LANES), 1) % GRID_W
    c0 = jnp.clip(c - NA_KW // 2, 0, GRID_W - NA_KW)
    valid = (kc >= c0) & (kc < c0 + NA_KW)
    for h in range(tab_ref.shape[0]):
        for dr in range(tab_ref.shape[1]):
            rows = jnp.broadcast_to(rpb_ref[h, dr:dr + 1, :], (GRID_W, LANES))
            skew = pltpu.roll(rows, LANES - (NA_KW - 1), 1, stride=1, stride_axis=0)
            tab_ref[h, dr] = jnp.where(valid, skew, MASK_VALUE)


def _na_kernel(q_ref, k_ref, v_ref, rpb_ref, o_ref, tab_ref, *s_refs, rows_per_step, n_rows):
    i = pl.program_id(2)
    win = NA_KH * GRID_W
    lo_mask = lax.broadcasted_iota(jnp.int32, (1, LANES), 1) < NA_HEAD_DIM
    zero = jnp.zeros((), BF16)

    @pl.when(i == 0)
    def _():
        _na_bias_tiles(rpb_ref, tab_ref)

    z_st = pl.multiple_of(jnp.minimum(pl.program_id(2), 0) * SUBLANES, SUBLANES)
    z_ld = pl.multiple_of(jnp.minimum(pl.program_id(1), 0) * SUBLANES, SUBLANES)
    st_rows, ld_rows = pl.ds(z_st, 2 * GRID_W), pl.ds(z_ld, 2 * GRID_W)

    def window(rr):
        r = i * rows_per_step + rr
        r0 = jnp.clip(r - NA_KH // 2, 0, n_rows - NA_KH)
        return r - r0, pl.ds(pl.multiple_of(r0 * GRID_W, GRID_W), win)

    def scores(rr):
        delta, keys = window(rr)
        q_row = q_ref[0, rr * GRID_W:(rr + 1) * GRID_W, :]
        q2 = jnp.concatenate([jnp.where(lo_mask, q_row, zero), jnp.where(lo_mask, zero, q_row)], axis=0)
        bias = jnp.concatenate(
            [jnp.concatenate([tab_ref[h, 2 * p + (NA_KH - 1) - delta] for p in range(NA_KH // 2)], axis=1)
             for h in range(2)], axis=0)
        s_refs[rr][st_rows, :] = _dot_nt(q2, k_ref[0, keys, :]) + bias

    def attend(rr):
        _, keys = window(rr)
        s_ref = s_refs[rr]
        m = jnp.max(s_ref[ld_rows, :], axis=-1, keepdims=True)
        e = jnp.exp(s_ref[ld_rows, :] - m)
        denom = jnp.sum(e, axis=-1, keepdims=True)
        o2 = _dot(e.astype(BF16), v_ref[0, keys, :]) * (1.0 / denom)
        out = jnp.where(lo_mask, o2[:GRID_W], o2[GRID_W:])
        o_ref[0, rr * GRID_W:(rr + 1) * GRID_W, :] = out.astype(BF16)

    for rr in range(min(NA_ROWS_AHEAD, rows_per_step)):
        scores(rr)
    for rr in range(rows_per_step):
        if rr + NA_ROWS_AHEAD < rows_per_step:
            scores(rr + NA_ROWS_AHEAD)
        attend(rr)


def _na_attn(q, k, v, rpb_pairs):
    B, S, D = q.shape
    n_rows = S // GRID_W
    tq = NA_ROWS_PER_STEP * GRID_W
    n_dr = rpb_pairs.shape[1]
    kv_spec = pl.BlockSpec((1, S, LANES), lambda b, j, i: (b, 0, j))
    return pl.pallas_call(
        functools.partial(_na_kernel, rows_per_step=NA_ROWS_PER_STEP, n_rows=n_rows),
        grid=(B, D // LANES, S // tq),
        in_specs=[
            pl.BlockSpec((1, tq, LANES), lambda b, j, i: (b, i, j)),
            kv_spec, kv_spec,
            pl.BlockSpec((2, n_dr, LANES), lambda b, j, i: (j, 0, 0)),
        ],
        out_specs=pl.BlockSpec((1, tq, LANES), lambda b, j, i: (b, i, j)),
        out_shape=jax.ShapeDtypeStruct((B, S, D), BF16),
        scratch_shapes=[pltpu.VMEM((2, n_dr, GRID_W, LANES), F32)]
        + [pltpu.VMEM((2 * GRID_W + SUBLANES, NA_KH * GRID_W), F32)] * NA_ROWS_PER_STEP,
        compiler_params=_params(),
        name="na_attn",
    )(q, k, v, rpb_pairs)


def _na_rpb_pairs(rpb):
    padded = jnp.pad(rpb, ((0, 0), (0, 0), (0, GRID_W - rpb.shape[-1])))
    return jnp.concatenate([padded[:, :-1], padded[:, 1:]], axis=-1)


def _pad_heads(w, heads, width):
    lead = w.shape[:-1]
    w = w.reshape(*lead, heads, width)
    w = jnp.pad(w, [(0, 0)] * len(lead) + [(0, 0), (0, LANES - width)])
    return w.reshape(*lead, heads * LANES)


def _rope_tables(seq_len):
    half = MLA_ROPE // 2
    pos = jnp.arange(seq_len, dtype=F32)
    freqs = ROPE_THETA ** (-jnp.arange(half, dtype=F32) / half)
    ang = pos[:, None] * freqs[None, :]
    cos, sin = jnp.cos(ang), jnp.sin(ang)
    z = lambda n: jnp.zeros((seq_len, n), F32)
    cos_t = jnp.concatenate([jnp.ones((seq_len, MLA_NOPE), F32), cos, cos, z(LANES - MLA_QK)], axis=1)
    sin_a = jnp.concatenate([z(MLA_NOPE), -sin, z(half), z(LANES - MLA_QK)], axis=1)
    sin_b = jnp.concatenate([z(MLA_NOPE), z(half), sin, z(LANES - MLA_QK)], axis=1)
    return cos_t, sin_a, sin_b, cos.T, sin.T


def kernel(x, mem, mix_norm_g, xattn_norm_g, ff_norm_g, w_mem_q, mem_q_g, w_mem_o, w_ff1, w_ff2, mem_tok_norm_g, w_mem_kv, mem_k_g, w_in_e, pool_w, pool_scale, q_lora_g, w_uq, kv_lora_g, w_ukv, mla_q_g, mla_k_g, w_out_e, w_qkv_o, na_q_g, na_k_g, na_rpb, w_out_o):
    B, S, D = x.shape
    row = lambda v: v.reshape(1, -1)

    mem_k, mem_v = _mem_kv(mem, row(mem_tok_norm_g), w_mem_kv.astype(BF16), row(mem_k_g))

    o1 = POOL_WIDTH
    o3 = o1 + Q_LORA + KV_LORA
    w_in = w_in_e[0]
    w_in_pad = jnp.concatenate(
        [w_in[:, :o3], jnp.zeros((D, MLA_NOPE), F32), w_in[:, o3:], jnp.zeros((D, LANES - MLA_QK), F32)],
        axis=1).astype(BF16)
    wuqt = _pad_heads(w_uq[0], MLA_HEADS, MLA_QK).T.astype(BF16)
    w_ukv_h = w_ukv[0].reshape(KV_LORA, MLA_HEADS, MLA_NOPE + MLA_V)
    wuk = _pad_heads(w_ukv_h[:, :, :MLA_NOPE].reshape(KV_LORA, -1), MLA_HEADS, MLA_NOPE).astype(BF16)
    wuvt = _pad_heads(w_ukv_h[:, :, MLA_NOPE:].reshape(KV_LORA, -1), MLA_HEADS, MLA_V).T.astype(BF16)
    qg = jnp.pad(mla_q_g[0], (0, LANES - MLA_QK)) * (MLA_QK ** -0.5 * LOG2_E)
    qg = jnp.broadcast_to(qg[:, None], (LANES, LANES))
    kg = jnp.pad(mla_k_g[0], (0, LANES - MLA_QK)).reshape(1, LANES)
    up, q, k, v = _l0_front(x, row(mix_norm_g[0]), w_in_pad, row(q_lora_g[0]), wuqt, row(kv_lora_g[0]),
                            wuk, wuvt, qg, kg, *_rope_tables(S))
    o = _mla_attn(q, k, v)
    w_out = w_out_e[0]
    wa = w_out[:POOL_WIDTH].astype(BF16)
    wb = jnp.pad(w_out[POOL_WIDTH:].reshape(MLA_HEADS, MLA_V, D), ((0, 0), (0, LANES - MLA_V), (0, 0)))
    wb = wb.reshape(MLA_HEADS * LANES, D).astype(BF16)
    def tail(layer):
        return (row(xattn_norm_g[layer]), w_mem_q[layer].astype(BF16), row(mem_q_g[layer]), mem_k, mem_v,
                w_mem_o[layer].astype(BF16), row(ff_norm_g[layer]), w_ff1[layer].astype(BF16),
                w_ff2[layer].astype(BF16))

    x = _l0_out(x, up, pool_w[0].astype(BF16), row(pool_scale[0]), wa, o, wb, tail(0))

    qg2 = jnp.tile(na_q_g[0], LANES // NA_HEAD_DIM).reshape(1, LANES)
    kg2 = jnp.tile(na_k_g[0], LANES // NA_HEAD_DIM).reshape(1, LANES)
    q, k, v = _l1_front(x, row(mix_norm_g[1]), w_qkv_o[0].astype(BF16), qg2, kg2)
    c = _na_attn(q, k, v, _na_rpb_pairs(na_rpb[0]))
    return _l1_out(x, c, w_out_o[0].astype(BF16), tail(1))
```

```python
import functools

import jax
import jax.numpy as jnp
from jax import lax
from jax.experimental import pallas as pl
from jax.experimental.pallas import tpu as pltpu

F32 = jnp.float32
BF16 = jnp.bfloat16

D_MODEL = 1024
GRID_W = 64
EPS = 1e-6
POOL_WIDTH = 512
POOL_GROUP_W = 128
POOL_WINDOWS = (2, 4, 8, 16)
POOL_HALO = 8
MLA_HEADS = 8
MLA_NOPE = 64
MLA_ROPE = 32
MLA_V = 64
MLA_QK = MLA_NOPE + MLA_ROPE
Q_LORA = 256
KV_LORA = 128
ROPE_THETA = 10000.0
NA_HEADS = 16
NA_HEAD_DIM = 64
NA_KH = 8
NA_KW = 16
MEM_HEADS = 4
MEM_HEAD_DIM = 256
D_FF = 4096

LANES = 128
SUBLANES = 8
MASK_VALUE = -1e30
LOG2_E = 1.4426950408889634
VMEM_LIMIT = 56 * 1024 * 1024

TOKEN_TILE = 512
OUT_SUB_TILES = 2
FF_CHUNK = 512
MLA_Q_TILE = 512
MLA_KV_TILE = 512
MLA_Q_TILES_PER_STEP = 2
MLA_VT_ROWS = 80
NA_ROWS_PER_STEP = 32
NA_ROWS_AHEAD = 4


def _const_spec(shape):
    zeros = (0,) * len(shape)
    return pl.BlockSpec(shape, lambda *_: zeros, pipeline_mode=pl.Buffered(1))


def _params(**flags):
    return pltpu.CompilerParams(vmem_limit_bytes=VMEM_LIMIT, flags=flags or None)


def _rms(x, g, n=None):
    n = x.shape[-1] if n is None else n
    ms = jnp.sum(x * x, axis=-1, keepdims=True) * (1.0 / n)
    return x * lax.rsqrt(ms + EPS) * g


def _dot(a, b):
    return jnp.dot(a, b, preferred_element_type=F32)


def _dot_nt(a, b):
    return lax.dot_general(a, b, (((1,), (1,)), ((), ())), preferred_element_type=F32)


def _mem_kv_kernel(mem_ref, g_ref, w_ref, kg_ref, k_out, v_out):
    mn = _rms(mem_ref[0], g_ref[...]).astype(BF16)
    kv = _dot(mn, w_ref[...])
    for h in range(MEM_HEADS):
        sl = slice(h * MEM_HEAD_DIM, (h + 1) * MEM_HEAD_DIM)
        k_out[0, :, sl] = _rms(kv[:, sl], kg_ref[...]).astype(BF16)
    v_out[0] = kv[:, D_MODEL:].astype(BF16)


def _mem_kv(mem, g, w_kv, k_g):
    B, N, D = mem.shape
    return pl.pallas_call(
        _mem_kv_kernel,
        grid=(B,),
        in_specs=[
            pl.BlockSpec((1, N, D), lambda b: (b, 0, 0)),
            _const_spec((1, D)),
            _const_spec((D, 2 * D)),
            _const_spec((1, MEM_HEAD_DIM)),
        ],
        out_specs=[pl.BlockSpec((1, N, D), lambda b: (b, 0, 0))] * 2,
        out_shape=[jax.ShapeDtypeStruct((B, N, D), BF16)] * 2,
        compiler_params=_params(),
        name="mem_kv",
    )(mem, g, w_kv, k_g)


def _l0_front_kernel(x_ref, g_ref, win_ref, qlg_ref, wuqt_ref, kvlg_ref, wuk_ref, wuvt_ref,
                     qg_ref, kg_ref, cos_ref, sa_ref, sb_ref, cost_ref, sint_ref,
                     up_ref, q_ref, k_ref, v_ref):
    subs = _sub_tiles(x_ref.shape[1])
    us = [_dot(_rms(x_ref[0, rows], g_ref[...]).astype(BF16), win_ref[...]) for rows in subs]
    o2 = POOL_WIDTH + Q_LORA
    o3 = o2 + KV_LORA
    half = MLA_ROPE // 2
    r1 = slice(MLA_NOPE, MLA_NOPE + half)
    r2 = slice(MLA_NOPE + half, MLA_QK)
    inv_n = 1.0 / MLA_QK
    for rows, u in zip(subs, us):
        n = rows.stop - rows.start
        up_ref[0, rows] = u[:, :POOL_WIDTH]
        cq = _rms(u[:, POOL_WIDTH:o2], qlg_ref[...])
        ckv = _rms(u[:, o2:o3], kvlg_ref[...])
        kr = u[:, o3:]
        qt = _dot(wuqt_ref[...], cq.T.astype(BF16))
        vt = _dot(wuvt_ref[...], ckv.T.astype(BF16))
        kf = _dot(ckv.astype(BF16), wuk_ref[...])

        cos_t, sin_t = cost_ref[:, rows], sint_ref[:, rows]
        gq = jnp.tile(qg_ref[...], (1, n // LANES))
        row_id = lax.broadcasted_iota(jnp.int32, (LANES, n), 0)
        for hd in range(MLA_HEADS):
            head = slice(hd * LANES, (hd + 1) * LANES)
            blk = qt[head]
            ms = jnp.sum(blk * blk, axis=0, keepdims=True) * inv_n
            y = blk * lax.rsqrt(ms + EPS) * gq
            y1, y2 = y[r1], y[r2]
            q_ref[0, hd, :, rows] = jnp.concatenate(
                [y[:MLA_NOPE], y1 * cos_t - y2 * sin_t, y1 * sin_t + y2 * cos_t, y[MLA_QK:]], axis=0).astype(BF16)
            v_ref[0, hd, :, rows] = jnp.where(row_id == MLA_V, 1.0, vt[head]).astype(BF16)

        krg = kr * kg_ref[...]
        partner = pltpu.roll(krg, LANES - half, 1) * sa_ref[rows] + pltpu.roll(krg, half, 1) * sb_ref[rows]
        gcos = kg_ref[...] * cos_ref[rows]
        for hd in range(MLA_HEADS):
            pre = kf[:, hd * LANES:(hd + 1) * LANES] + kr
            ms = jnp.sum(pre * pre, axis=-1, keepdims=True) * inv_n
            k_ref[0, hd, rows] = ((pre * gcos + partner) * lax.rsqrt(ms + EPS)).astype(BF16)


def _l0_front(x, g, w_in, qlg, wuqt, kvlg, wuk, wuvt, qg, kg, cos, sa, sb, cos_t, sin_t):
    B, S, D = x.shape
    tm = TOKEN_TILE
    hw = MLA_HEADS * LANES
    head_spec = pl.BlockSpec((1, MLA_HEADS, tm, LANES), lambda b, i: (b, 0, i, 0))
    head_t_spec = pl.BlockSpec((1, MLA_HEADS, LANES, tm), lambda b, i: (b, 0, 0, i))
    tab_spec = pl.BlockSpec((tm, LANES), lambda b, i: (i, 0))
    tab_t_spec = pl.BlockSpec((MLA_ROPE // 2, tm), lambda b, i: (0, i))
    return pl.pallas_call(
        _l0_front_kernel,
        grid=(B, S // tm),
        in_specs=[
            pl.BlockSpec((1, tm, D), lambda b, i: (b, i, 0)),
            _const_spec((1, D)),
            _const_spec((D, D)),
            _const_spec((1, Q_LORA)),
            _const_spec((hw, Q_LORA)),
            _const_spec((1, KV_LORA)),
            _const_spec((KV_LORA, hw)),
            _const_spec((hw, KV_LORA)),
            _const_spec((LANES, LANES)),
            _const_spec((1, LANES)),
            tab_spec, tab_spec, tab_spec, tab_t_spec, tab_t_spec,
        ],
        out_specs=[
            pl.BlockSpec((1, tm, POOL_WIDTH), lambda b, i: (b, i, 0)),
            head_t_spec, head_spec, head_t_spec,
        ],
        out_shape=[
            jax.ShapeDtypeStruct((B, S, POOL_WIDTH), F32),
            jax.ShapeDtypeStruct((B, MLA_HEADS, LANES, S), BF16),
            jax.ShapeDtypeStruct((B, MLA_HEADS, S, LANES), BF16),
            jax.ShapeDtypeStruct((B, MLA_HEADS, LANES, S), BF16),
        ],
        compiler_params=_params(),
        name="l0_front",
    )(x, g, w_in, qlg, wuqt, kvlg, wuk, wuvt, qg, kg, cos, sa, sb, cos_t, sin_t)


def _mla_attn_kernel(qt_ref, k_ref, vt_ref, o_ref, s0_ref, s1_ref, s2_ref, acc0_ref, acc1_ref, *, tq, tk):
    nk = k_ref.shape[2] // tk
    nq = qt_ref.shape[3] // tq
    bufs = (s0_ref, s1_ref, s2_ref)
    accs = (acc0_ref, acc1_ref)
    chunk_max = {}
    z_st = pl.multiple_of(jnp.minimum(pl.program_id(2), 0) * SUBLANES, SUBLANES)
    z_ld = pl.multiple_of(jnp.minimum(pl.program_id(1), 0) * SUBLANES, SUBLANES)
    st_rows, ld_rows = pl.ds(z_st, tk), pl.ds(z_ld, tk)
    acc_st, acc_ld = pl.ds(z_st, MLA_VT_ROWS), pl.ds(z_ld, MLA_VT_ROWS)

    def fill(v):
        t, c = divmod(v, nk)
        qt = qt_ref[0, 0, :, t * tq:(t + 1) * tq]
        s = _dot(k_ref[0, 0, c * tk:(c + 1) * tk, :], qt)
        bufs[v % 3][st_rows, :] = s
        chunk_max[v] = jnp.max(s, axis=0, keepdims=True)

    def update(v, m_old):
        t, c = divmod(v, nk)
        s_ref, acc_ref = bufs[v % 3], accs[t % 2]
        vt = vt_ref[0, 0, :MLA_VT_ROWS, c * tk:(c + 1) * tk]
        cm = chunk_max.pop(v)
        m_new = cm if c == 0 else jnp.maximum(m_old, cm)
        p = jnp.exp2((s_ref[ld_rows, :] - m_new).astype(BF16))
        pv = _dot(vt, p)
        if c == 0:
            acc_ref[acc_st, :] = pv
        else:
            acc_ref[acc_st, :] = acc_ref[acc_ld, :] * jnp.exp2(m_old - m_new) + pv
        if c == nk - 1:
            acc = acc_ref[acc_ld, :]
            out = acc[:MLA_V] * (1.0 / acc[MLA_V:MLA_V + 1])
            out = jnp.concatenate([out, jnp.zeros((LANES - MLA_V, tq), F32)], axis=0)
            o_ref[0, t * tq:(t + 1) * tq, :] = out.T.astype(BF16)
        return m_new

    fill(0)
    fill(1)
    m = None
    for v in range(nq * nk):
        if v + 2 < nq * nk:
            fill(v + 2)
        m = update(v, m)


def _mla_attn(qt, k, vt):
    B, H, S, L = k.shape
    tq, tk, nq = MLA_Q_TILE, MLA_KV_TILE, MLA_Q_TILES_PER_STEP
    assert S % tk == 0 and S % (nq * tq) == 0
    return pl.pallas_call(
        functools.partial(_mla_attn_kernel, tq=tq, tk=tk),
        grid=(B, H, S // (nq * tq)),
        in_specs=[
            pl.BlockSpec((1, 1, L, nq * tq), lambda b, h, i: (b, h, 0, i)),
            pl.BlockSpec((1, 1, S, L), lambda b, h, i: (b, h, 0, 0)),
            pl.BlockSpec((1, 1, L, S), lambda b, h, i: (b, h, 0, 0)),
        ],
        out_specs=pl.BlockSpec((1, nq * tq, L), lambda b, h, i: (b, i, h)),
        out_shape=jax.ShapeDtypeStruct((B, S, H * L), BF16),
        scratch_shapes=[pltpu.VMEM((tk + SUBLANES, tq), F32)] * 3
        + [pltpu.VMEM((MLA_VT_ROWS + SUBLANES, tq), F32)] * 2,
        compiler_params=_params(),
        name="mla_attn",
    )(qt, k, vt)


def _pool_mix(up_ref, upp_ref, upn_ref, pw_ref, ps_ref, ext_ref, tm, seq_len):
    i = pl.program_id(1)
    last = pl.num_programs(1) - 1
    ext_ref[0:POOL_HALO] = jnp.where(i > 0, upp_ref[0], 0.0)
    ext_ref[POOL_HALO:POOL_HALO + tm] = up_ref[0]
    ext_ref[POOL_HALO + tm:2 * POOL_HALO + tm] = jnp.where(i < last, upn_ref[0], 0.0)
    t = i * tm + lax.broadcasted_iota(jnp.int32, (tm, 1), 0)
    outs = []
    for g, w in enumerate(POOL_WINDOWS):
        cols = slice(g * POOL_GROUP_W, (g + 1) * POOL_GROUP_W)
        wsum = None
        for o in range(-(w // 2), w - w // 2):
            piece = ext_ref[POOL_HALO + o:POOL_HALO + o + tm, cols]
            wsum = piece if wsum is None else wsum + piece
        lo = jnp.maximum(t - w // 2, 0)
        hi = jnp.minimum(t + (w - 1 - w // 2), seq_len - 1)
        cnt = (hi - lo + 1).astype(F32)
        d = wsum / cnt - ext_ref[POOL_HALO:POOL_HALO + tm, cols]
        outs.append(_dot(d.astype(BF16), pw_ref[g]))
    return jnp.concatenate(outs, axis=-1) * ps_ref[...]


def _mem_xattn(x1s, xg_ref, wq_ref, qg_ref, mk_ref, mv_ref, wo_ref):
    scale = MEM_HEAD_DIM ** -0.5 * LOG2_E
    heads = [slice(hd * MEM_HEAD_DIM, (hd + 1) * MEM_HEAD_DIM) for hd in range(MEM_HEADS)]
    qms = [_dot(_rms(x1, xg_ref[...]).astype(BF16), wq_ref[...]) for x1 in x1s]
    scores = [[_dot_nt((_rms(qm[:, sl], qg_ref[...]) * scale).astype(BF16), mk_ref[0, :, sl]) for sl in heads]
              for qm in qms]
    outs = []
    for x1, per_head in zip(x1s, scores):
        ohs = []
        for s, sl in zip(per_head, heads):
            e = jnp.exp2(s - jnp.max(s, axis=-1, keepdims=True))
            denom = jnp.sum(e, axis=-1, keepdims=True)
            ohs.append((_dot(e.astype(BF16), mv_ref[0, :, sl]) * (1.0 / denom)).astype(BF16))
        outs.append(x1 + _dot(jnp.concatenate(ohs, axis=-1), wo_ref[...]))
    return outs


def _sub_tiles(tm):
    step = tm // OUT_SUB_TILES
    return [slice(r * step, (r + 1) * step) for r in range(OUT_SUB_TILES)]


def _mlp(x2s, fg_ref, w1_ref, w2_ref):
    outs = []
    for x2 in x2s:
        h = _rms(x2, fg_ref[...]).astype(BF16)
        acc = x2
        for c in range(D_FF // FF_CHUNK):
            sl = slice(c * FF_CHUNK, (c + 1) * FF_CHUNK)
            a = jnp.maximum(_dot(h, w1_ref[:, sl]), 0.0)
            acc = acc + _dot((a * a).astype(BF16), w2_ref[sl, :])
        outs.append(acc)
    return outs


def _layer_tail(x1s, tail_refs, out_ref):
    xg_ref, wq_ref, qg_ref, mk_ref, mv_ref, wo_ref, fg_ref, w1_ref, w2_ref = tail_refs
    x2s = _mem_xattn(x1s, xg_ref, wq_ref, qg_ref, mk_ref, mv_ref, wo_ref)
    out_ref[0] = _mlp([jnp.concatenate(x2s, axis=0)], fg_ref, w1_ref, w2_ref)[0]


def _l0_out_kernel(x_ref, up_ref, upp_ref, upn_ref, pw_ref, ps_ref, wa_ref, o_ref, wb_ref, *rest, tm, seq_len):
    *tail_refs, out_ref, ext_ref = rest
    a = _pool_mix(up_ref, upp_ref, upn_ref, pw_ref, ps_ref, ext_ref, tm, seq_len).astype(BF16)
    subs = _sub_tiles(tm)
    x1s = [x_ref[0, rows] + _dot(a[rows], wa_ref[...]) + _dot(o_ref[0, rows], wb_ref[...]) for rows in subs]
    _layer_tail(x1s, tail_refs, out_ref)


def _l1_out_kernel(x_ref, o_ref, wb_ref, *rest, tm):
    *tail_refs, out_ref = rest
    subs = _sub_tiles(tm)
    x1s = [x_ref[0, rows] + _dot(o_ref[0, rows], wb_ref[...]) for rows in subs]
    _layer_tail(x1s, tail_refs, out_ref)


def _tail_specs(n_mem):
    D = D_MODEL
    return [
        _const_spec((1, D)),
        _const_spec((D, D)),
        _const_spec((1, MEM_HEAD_DIM)),
        pl.BlockSpec((1, n_mem, D), lambda b, i: (b, 0, 0), pipeline_mode=pl.Buffered(1)),
        pl.BlockSpec((1, n_mem, D), lambda b, i: (b, 0, 0), pipeline_mode=pl.Buffered(1)),
        _const_spec((D, D)),
        _const_spec((1, D)),
        _const_spec((D, D_FF)),
        _const_spec((D_FF, D)),
    ]


def _l0_out(x, up, pool_w, pool_scale, wa, o, wb, tail):
    B, S, D = x.shape
    n_mem = tail[3].shape[1]
    tm = TOKEN_TILE
    halo_blocks = tm // POOL_HALO
    tile = pl.BlockSpec((1, tm, D), lambda b, i: (b, i, 0))
    return pl.pallas_call(
        functools.partial(_l0_out_kernel, tm=tm, seq_len=S),
        grid=(B, S // tm),
        in_specs=[
            tile,
            pl.BlockSpec((1, tm, POOL_WIDTH), lambda b, i: (b, i, 0)),
            pl.BlockSpec((1, POOL_HALO, POOL_WIDTH),
                         lambda b, i: (b, jnp.maximum(i * halo_blocks - 1, 0), 0)),
            pl.BlockSpec((1, POOL_HALO, POOL_WIDTH),
                         lambda b, i: (b, jnp.minimum((i + 1) * halo_blocks, S // POOL_HALO - 1), 0)),
            _const_spec(pool_w.shape),
            _const_spec((1, POOL_WIDTH)),
            _const_spec((POOL_WIDTH, D)),
            pl.BlockSpec((1, tm, o.shape[-1]), lambda b, i: (b, i, 0)),
            _const_spec(wb.shape),
        ] + _tail_specs(n_mem),
        out_specs=tile,
        out_shape=jax.ShapeDtypeStruct((B, S, D), F32),
        scratch_shapes=[pltpu.VMEM((tm + 2 * POOL_HALO, POOL_WIDTH), F32)],
        compiler_params=_params(),
        name="l0_tail",
    )(x, up, up, up, pool_w, pool_scale, wa, o, wb, *tail)


def _l1_out(x, o, wb, tail):
    B, S, D = x.shape
    n_mem = tail[3].shape[1]
    tm = TOKEN_TILE
    tile = pl.BlockSpec((1, tm, D), lambda b, i: (b, i, 0))
    return pl.pallas_call(
        functools.partial(_l1_out_kernel, tm=tm),
        grid=(B, S // tm),
        in_specs=[tile, pl.BlockSpec((1, tm, o.shape[-1]), lambda b, i: (b, i, 0)), _const_spec(wb.shape)]
        + _tail_specs(n_mem),
        out_specs=tile,
        out_shape=jax.ShapeDtypeStruct((B, S, D), F32),
        compiler_params=_params(),
        name="l1_tail",
    )(x, o, wb, *tail)


def _pair_norm(z, g2, lo_mask):
    y = z * z
    lo = jnp.sum(jnp.where(lo_mask, y, 0.0), axis=-1, keepdims=True)
    hi = jnp.sum(jnp.where(lo_mask, 0.0, y), axis=-1, keepdims=True)
    inv = 1.0 / NA_HEAD_DIM
    r = jnp.where(lo_mask, lax.rsqrt(lo * inv + EPS), lax.rsqrt(hi * inv + EPS))
    return z * r * g2


def _l1_front_kernel(x_ref, g_ref, w_ref, qg_ref, kg_ref, q_out, k_out, v_out):
    h = _rms(x_ref[0], g_ref[...]).astype(BF16)
    qkv = _dot(h, w_ref[...])
    width = NA_HEADS * NA_HEAD_DIM
    lo_mask = lax.broadcasted_iota(jnp.int32, (1, LANES), 1) < NA_HEAD_DIM
    scale = NA_HEAD_DIM ** -0.5 * LOG2_E
    for j in range(width // LANES):
        sl = slice(j * LANES, (j + 1) * LANES)
        q_out[0, :, sl] = (_pair_norm(qkv[:, sl], qg_ref[...], lo_mask) * scale).astype(BF16)
        ksl = slice(width + j * LANES, width + (j + 1) * LANES)
        k_out[0, :, sl] = _pair_norm(qkv[:, ksl], kg_ref[...], lo_mask).astype(BF16)
    v_out[0] = qkv[:, 2 * width:].astype(BF16)


def _l1_front(x, g, w_qkv, qg2, kg2):
    B, S, D = x.shape
    tm = TOKEN_TILE
    tile = pl.BlockSpec((1, tm, D), lambda b, i: (b, i, 0))
    return pl.pallas_call(
        _l1_front_kernel,
        grid=(B, S // tm),
        in_specs=[tile, _const_spec((1, D)), _const_spec((D, 3 * D)), _const_spec((1, LANES)),
                  _const_spec((1, LANES))],
        out_specs=[tile] * 3,
        out_shape=[jax.ShapeDtypeStruct((B, S, D), BF16)] * 3,
        compiler_params=_params(),
        name="l1_front",
    )(x, g, w_qkv, qg2, kg2)


def _na_bias_tiles(rpb_ref, tab_ref):
    c = lax.broadcasted_iota(jnp.int32, (GRID_W, LANES), 0)
    kc = lax.broadcasted_iota(jnp.int32, (GRID_W, LANES), 1) % GRID_W
    c0 = jnp.clip(c - NA_KW // 2, 0, GRID_W - NA_KW)
    valid = (kc >= c0) & (kc < c0 + NA_KW)
    for h in range(tab_ref.shape[0]):
        for dr in range(tab_ref.shape[1]):
            rows = jnp.broadcast_to(rpb_ref[h, dr:dr + 1, :], (GRID_W, LANES))
            skew = pltpu.roll(rows, LANES - (NA_KW - 1), 1, stride=1, stride_axis=0)
            tab_ref[h, dr] = jnp.where(valid, skew * LOG2_E, MASK_VALUE)


def _na_kernel(q_ref, k_ref, v_ref, rpb_ref, o_ref, tab_ref, *s_refs, rows_per_step, n_rows):
    i = pl.program_id(2)
    win = NA_KH * GRID_W
    lo_mask = lax.broadcasted_iota(jnp.int32, (1, LANES), 1) < NA_HEAD_DIM
    zero = jnp.zeros((), BF16)

    @pl.when(i == 0)
    def _():
        _na_bias_tiles(rpb_ref, tab_ref)

    z_st = pl.multiple_of(jnp.minimum(pl.program_id(2), 0) * SUBLANES, SUBLANES)
    z_ld = pl.multiple_of(jnp.minimum(pl.program_id(1), 0) * SUBLANES, SUBLANES)
    st_rows, ld_rows = pl.ds(z_st, 2 * GRID_W), pl.ds(z_ld, 2 * GRID_W)
    row_max = {}

    def window(rr):
        r = i * rows_per_step + rr
        r0 = jnp.clip(r - NA_KH // 2, 0, n_rows - NA_KH)
        return r - r0, pl.ds(pl.multiple_of(r0 * GRID_W, GRID_W), win)

    def scores(rr):
        delta, keys = window(rr)
        q_row = q_ref[0, rr * GRID_W:(rr + 1) * GRID_W, :]
        q2 = jnp.concatenate([jnp.where(lo_mask, q_row, zero), jnp.where(lo_mask, zero, q_row)], axis=0)
        bias = jnp.concatenate(
            [jnp.concatenate([tab_ref[h, 2 * p + (NA_KH - 1) - delta] for p in range(NA_KH // 2)], axis=1)
             for h in range(2)], axis=0)
        s = _dot_nt(q2, k_ref[0, keys, :]) + bias
        s_refs[rr][st_rows, :] = s
        row_max[rr] = jnp.max(s, axis=-1, keepdims=True)

    def attend(rr):
        _, keys = window(rr)
        e = jnp.exp2(s_refs[rr][ld_rows, :] - row_max.pop(rr))
        denom = jnp.sum(e, axis=-1, keepdims=True)
        o2 = _dot(e.astype(BF16), v_ref[0, keys, :]) * (1.0 / denom)
        out = jnp.where(lo_mask, o2[:GRID_W], o2[GRID_W:])
        o_ref[0, rr * GRID_W:(rr + 1) * GRID_W, :] = out.astype(BF16)

    for rr in range(min(NA_ROWS_AHEAD, rows_per_step)):
        scores(rr)
    for rr in range(rows_per_step):
        if rr + NA_ROWS_AHEAD < rows_per_step:
            scores(rr + NA_ROWS_AHEAD)
        attend(rr)


def _na_attn(q, k, v, rpb_pairs):
    B, S, D = q.shape
    n_rows = S // GRID_W
    tq = NA_ROWS_PER_STEP * GRID_W
    n_dr = rpb_pairs.shape[1]
    kv_spec = pl.BlockSpec((1, S, LANES), lambda b, j, i: (b, 0, j))
    return pl.pallas_call(
        functools.partial(_na_kernel, rows_per_step=NA_ROWS_PER_STEP, n_rows=n_rows),
        grid=(B, D // LANES, S // tq),
        in_specs=[
            pl.BlockSpec((1, tq, LANES), lambda b, j, i: (b, i, j)),
            kv_spec, kv_spec,
            pl.BlockSpec((2, n_dr, LANES), lambda b, j, i: (j, 0, 0)),
        ],
        out_specs=pl.BlockSpec((1, tq, LANES), lambda b, j, i: (b, i, j)),
        out_shape=jax.ShapeDtypeStruct((B, S, D), BF16),
        scratch_shapes=[pltpu.VMEM((2, n_dr, GRID_W, LANES), F32)]
        + [pltpu.VMEM((2 * GRID_W + SUBLANES, NA_KH * GRID_W), F32)] * NA_ROWS_PER_STEP,
        compiler_params=_params(),
        name="na_attn",
    )(q, k, v, rpb_pairs)


def _na_rpb_pairs(rpb):
    padded = jnp.pad(rpb, ((0, 0), (0, 0), (0, GRID_W - rpb.shape[-1])))
    return jnp.concatenate([padded[:, :-1], padded[:, 1:]], axis=-1)


def _pad_heads(w, heads, width):
    lead = w.shape[:-1]
    w = w.reshape(*lead, heads, width)
    w = jnp.pad(w, [(0, 0)] * len(lead) + [(0, 0), (0, LANES - width)])
    return w.reshape(*lead, heads * LANES)


def _rope_tables(seq_len):
    half = MLA_ROPE // 2
    pos = jnp.arange(seq_len, dtype=F32)
    freqs = ROPE_THETA ** (-jnp.arange(half, dtype=F32) / half)
    ang = pos[:, None] * freqs[None, :]
    cos, sin = jnp.cos(ang), jnp.sin(ang)
    z = lambda n: jnp.zeros((seq_len, n), F32)
    cos_t = jnp.concatenate([jnp.ones((seq_len, MLA_NOPE), F32), cos, cos, z(LANES - MLA_QK)], axis=1)
    sin_a = jnp.concatenate([z(MLA_NOPE), -sin, z(half), z(LANES - MLA_QK)], axis=1)
    sin_b = jnp.concatenate([z(MLA_NOPE), z(half), sin, z(LANES - MLA_QK)], axis=1)
    return cos_t, sin_a, sin_b, cos.T, sin.T


def kernel(x, mem, mix_norm_g, xattn_norm_g, ff_norm_g, w_mem_q, mem_q_g, w_mem_o, w_ff1, w_ff2, mem_tok_norm_g, w_mem_kv, mem_k_g, w_in_e, pool_w, pool_scale, q_lora_g, w_uq, kv_lora_g, w_ukv, mla_q_g, mla_k_g, w_out_e, w_qkv_o, na_q_g, na_k_g, na_rpb, w_out_o):
    B, S, D = x.shape
    row = lambda v: v.reshape(1, -1)

    mem_k, mem_v = _mem_kv(mem, row(mem_tok_norm_g), w_mem_kv.astype(BF16), row(mem_k_g))

    o1 = POOL_WIDTH
    o3 = o1 + Q_LORA + KV_LORA
    w_in = w_in_e[0]
    w_in_pad = jnp.concatenate(
        [w_in[:, :o3], jnp.zeros((D, MLA_NOPE), F32), w_in[:, o3:], jnp.zeros((D, LANES - MLA_QK), F32)],
        axis=1).astype(BF16)
    wuqt = _pad_heads(w_uq[0], MLA_HEADS, MLA_QK).T.astype(BF16)
    w_ukv_h = w_ukv[0].reshape(KV_LORA, MLA_HEADS, MLA_NOPE + MLA_V)
    wuk = _pad_heads(w_ukv_h[:, :, :MLA_NOPE].reshape(KV_LORA, -1), MLA_HEADS, MLA_NOPE).astype(BF16)
    wuvt = _pad_heads(w_ukv_h[:, :, MLA_NOPE:].reshape(KV_LORA, -1), MLA_HEADS, MLA_V).T.astype(BF16)
    qg = jnp.pad(mla_q_g[0], (0, LANES - MLA_QK)) * (MLA_QK ** -0.5 * LOG2_E)
    qg = jnp.broadcast_to(qg[:, None], (LANES, LANES))
    kg = jnp.pad(mla_k_g[0], (0, LANES - MLA_QK)).reshape(1, LANES)
    up, q, k, v = _l0_front(x, row(mix_norm_g[0]), w_in_pad, row(q_lora_g[0]), wuqt, row(kv_lora_g[0]),
                            wuk, wuvt, qg, kg, *_rope_tables(S))
    o = _mla_attn(q, k, v)
    w_out = w_out_e[0]
    wa = w_out[:POOL_WIDTH].astype(BF16)
    wb = jnp.pad(w_out[POOL_WIDTH:].reshape(MLA_HEADS, MLA_V, D), ((0, 0), (0, LANES - MLA_V), (0, 0)))
    wb = wb.reshape(MLA_HEADS * LANES, D).astype(BF16)
    def tail(layer):
        return (row(xattn_norm_g[layer]), w_mem_q[layer].astype(BF16), row(mem_q_g[layer]), mem_k, mem_v,
                w_mem_o[layer].astype(BF16), row(ff_norm_g[layer]), w_ff1[layer].astype(BF16),
                w_ff2[layer].astype(BF16))

    x = _l0_out(x, up, pool_w[0].astype(BF16), row(pool_scale[0]), wa, o, wb, tail(0))

    qg2 = jnp.tile(na_q_g[0], LANES // NA_HEAD_DIM).reshape(1, LANES)
    kg2 = jnp.tile(na_k_g[0], LANES // NA_HEAD_DIM).reshape(1, LANES)
    q, k, v = _l1_front(x, row(mix_norm_g[1]), w_qkv_o[0].astype(BF16), qg2, kg2)
    c = _na_attn(q, k, v, _na_rpb_pairs(na_rpb[0]))
    return _l1_out(x, c, w_out_o[0].astype(BF16), tail(1))
```

```python
import functools

import jax
import jax.numpy as jnp
from jax import lax
from jax.experimental import pallas as pl
from jax.experimental.pallas import tpu as pltpu

F32 = jnp.float32
BF16 = jnp.bfloat16

D_MODEL = 1024
GRID_W = 64
EPS = 1e-6
POOL_WIDTH = 512
POOL_GROUP_W = 128
POOL_WINDOWS = (2, 4, 8, 16)
POOL_HALO = 8
MLA_HEADS = 8
MLA_NOPE = 64
MLA_ROPE = 32
MLA_V = 64
MLA_QK = MLA_NOPE + MLA_ROPE
Q_LORA = 256
KV_LORA = 128
ROPE_THETA = 10000.0
NA_HEADS = 16
NA_HEAD_DIM = 64
NA_KH = 8
NA_KW = 16
MEM_HEADS = 4
MEM_HEAD_DIM = 256
D_FF = 4096

LANES = 128
SUBLANES = 8
MASK_VALUE = -1e30
LOG2_E = 1.4426950408889634
VMEM_LIMIT = 56 * 1024 * 1024

TOKEN_TILE = 512
OUT_SUB_TILES = 2
FF_CHUNK = 512
MLA_Q_TILE = 512
MLA_KV_TILE = 512
MLA_Q_TILES_PER_STEP = 4
MLA_VT_ROWS = 80
NA_ROWS_PER_STEP = 32
NA_ROWS_AHEAD = 4


def _const_spec(shape):
    zeros = (0,) * len(shape)
    return pl.BlockSpec(shape, lambda *_: zeros, pipeline_mode=pl.Buffered(1))


def _params(**flags):
    return pltpu.CompilerParams(vmem_limit_bytes=VMEM_LIMIT, flags=flags or None)


def _rms(x, g, n=None):
    n = x.shape[-1] if n is None else n
    ms = jnp.sum(x * x, axis=-1, keepdims=True) * (1.0 / n)
    return x * lax.rsqrt(ms + EPS) * g


def _dot(a, b):
    return jnp.dot(a, b, preferred_element_type=F32)


def _dot_nt(a, b):
    return lax.dot_general(a, b, (((1,), (1,)), ((), ())), preferred_element_type=F32)


def _mem_kv_kernel(mem_ref, g_ref, w_ref, kg_ref, k_out, v_out):
    mn = _rms(mem_ref[0], g_ref[...]).astype(BF16)
    kv = _dot(mn, w_ref[...])
    for h in range(MEM_HEADS):
        sl = slice(h * MEM_HEAD_DIM, (h + 1) * MEM_HEAD_DIM)
        k_out[0, :, sl] = _rms(kv[:, sl], kg_ref[...]).astype(BF16)
    v_out[0] = kv[:, D_MODEL:].astype(BF16)


def _mem_kv(mem, g, w_kv, k_g):
    B, N, D = mem.shape
    return pl.pallas_call(
        _mem_kv_kernel,
        grid=(B,),
        in_specs=[
            pl.BlockSpec((1, N, D), lambda b: (b, 0, 0)),
            _const_spec((1, D)),
            _const_spec((D, 2 * D)),
            _const_spec((1, MEM_HEAD_DIM)),
        ],
        out_specs=[pl.BlockSpec((1, N, D), lambda b: (b, 0, 0))] * 2,
        out_shape=[jax.ShapeDtypeStruct((B, N, D), BF16)] * 2,
        compiler_params=_params(),
        name="mem_kv",
    )(mem, g, w_kv, k_g)


def _l0_front_kernel(x_ref, g_ref, win_ref, qlg_ref, wuqt_ref, kvlg_ref, wuk_ref, wuvt_ref,
                     qg_ref, kg_ref, cos_ref, sa_ref, sb_ref, cost_ref, sint_ref,
                     up_ref, q_ref, k_ref, v_ref):
    subs = _sub_tiles(x_ref.shape[1])
    us = [_dot(_rms(x_ref[0, rows], g_ref[...]).astype(BF16), win_ref[...]) for rows in subs]
    o2 = POOL_WIDTH + Q_LORA
    o3 = o2 + KV_LORA
    half = MLA_ROPE // 2
    r1 = slice(MLA_NOPE, MLA_NOPE + half)
    r2 = slice(MLA_NOPE + half, MLA_QK)
    inv_n = 1.0 / MLA_QK
    for rows, u in zip(subs, us):
        n = rows.stop - rows.start
        up_ref[0, rows] = u[:, :POOL_WIDTH]
        cq = _rms(u[:, POOL_WIDTH:o2], qlg_ref[...])
        ckv = _rms(u[:, o2:o3], kvlg_ref[...])
        kr = u[:, o3:]
        qt = _dot(wuqt_ref[...], cq.T.astype(BF16))
        vt = _dot(wuvt_ref[...], ckv.T.astype(BF16))
        kf = _dot(ckv.astype(BF16), wuk_ref[...])

        cos_t, sin_t = cost_ref[:, rows], sint_ref[:, rows]
        gq = jnp.tile(qg_ref[...], (1, n // LANES))
        row_id = lax.broadcasted_iota(jnp.int32, (LANES, n), 0)
        for hd in range(MLA_HEADS):
            head = slice(hd * LANES, (hd + 1) * LANES)
            blk = qt[head]
            ms = jnp.sum(blk * blk, axis=0, keepdims=True) * inv_n
            y = blk * lax.rsqrt(ms + EPS) * gq
            y1, y2 = y[r1], y[r2]
            q_ref[0, hd, :, rows] = jnp.concatenate(
                [y[:MLA_NOPE], y1 * cos_t - y2 * sin_t, y1 * sin_t + y2 * cos_t, y[MLA_QK:]], axis=0).astype(BF16)
            v_ref[0, hd, :, rows] = jnp.where(row_id == MLA_V, 1.0, vt[head]).astype(BF16)

        krg = kr * kg_ref[...]
        partner = pltpu.roll(krg, LANES - half, 1) * sa_ref[rows] + pltpu.roll(krg, half, 1) * sb_ref[rows]
        gcos = kg_ref[...] * cos_ref[rows]
        for hd in range(MLA_HEADS):
            pre = kf[:, hd * LANES:(hd + 1) * LANES] + kr
            ms = jnp.sum(pre * pre, axis=-1, keepdims=True) * inv_n
            k_ref[0, hd, rows] = ((pre * gcos + partner) * lax.rsqrt(ms + EPS)).astype(BF16)


def _l0_front(x, g, w_in, qlg, wuqt, kvlg, wuk, wuvt, qg, kg, cos, sa, sb, cos_t, sin_t):
    B, S, D = x.shape
    tm = TOKEN_TILE
    hw = MLA_HEADS * LANES
    head_spec = pl.BlockSpec((1, MLA_HEADS, tm, LANES), lambda b, i: (b, 0, i, 0))
    head_t_spec = pl.BlockSpec((1, MLA_HEADS, LANES, tm), lambda b, i: (b, 0, 0, i))
    tab_spec = pl.BlockSpec((tm, LANES), lambda b, i: (i, 0))
    tab_t_spec = pl.BlockSpec((MLA_ROPE // 2, tm), lambda b, i: (0, i))
    return pl.pallas_call(
        _l0_front_kernel,
        grid=(B, S // tm),
        in_specs=[
            pl.BlockSpec((1, tm, D), lambda b, i: (b, i, 0)),
            _const_spec((1, D)),
            _const_spec((D, D)),
            _const_spec((1, Q_LORA)),
            _const_spec((hw, Q_LORA)),
            _const_spec((1, KV_LORA)),
            _const_spec((KV_LORA, hw)),
            _const_spec((hw, KV_LORA)),
            _const_spec((LANES, LANES)),
            _const_spec((1, LANES)),
            tab_spec, tab_spec, tab_spec, tab_t_spec, tab_t_spec,
        ],
        out_specs=[
            pl.BlockSpec((1, tm, POOL_WIDTH), lambda b, i: (b, i, 0)),
            head_t_spec, head_spec, head_t_spec,
        ],
        out_shape=[
            jax.ShapeDtypeStruct((B, S, POOL_WIDTH), F32),
            jax.ShapeDtypeStruct((B, MLA_HEADS, LANES, S), BF16),
            jax.ShapeDtypeStruct((B, MLA_HEADS, S, LANES), BF16),
            jax.ShapeDtypeStruct((B, MLA_HEADS, LANES, S), BF16),
        ],
        compiler_params=_params(),
        name="l0_front",
    )(x, g, w_in, qlg, wuqt, kvlg, wuk, wuvt, qg, kg, cos, sa, sb, cos_t, sin_t)


def _mla_attn_kernel(qt_ref, k_ref, vt_ref, o_ref, s0_ref, s1_ref, s2_ref, acc0_ref, acc1_ref, *, tq, tk):
    nk = k_ref.shape[2] // tk
    nq = qt_ref.shape[3] // tq
    bufs = (s0_ref, s1_ref, s2_ref)
    accs = (acc0_ref, acc1_ref)
    chunk_max = {}
    z_st = pl.multiple_of(jnp.minimum(pl.program_id(2), 0) * SUBLANES, SUBLANES)
    z_ld = pl.multiple_of(jnp.minimum(pl.program_id(1), 0) * SUBLANES, SUBLANES)
    st_rows, ld_rows = pl.ds(z_st, tk), pl.ds(z_ld, tk)
    acc_st, acc_ld = pl.ds(z_st, MLA_VT_ROWS), pl.ds(z_ld, MLA_VT_ROWS)

    def fill(v):
        t, c = divmod(v, nk)
        qt = qt_ref[0, 0, :, t * tq:(t + 1) * tq]
        s = _dot(k_ref[0, 0, c * tk:(c + 1) * tk, :], qt)
        bufs[v % 3][st_rows, :] = s
        chunk_max[v] = jnp.max(s, axis=0, keepdims=True)

    def update(v, m_old):
        t, c = divmod(v, nk)
        s_ref, acc_ref = bufs[v % 3], accs[t % 2]
        vt = vt_ref[0, 0, :MLA_VT_ROWS, c * tk:(c + 1) * tk]
        cm = chunk_max.pop(v)
        m_new = cm if c == 0 else jnp.maximum(m_old, cm)
        p = jnp.exp2((s_ref[ld_rows, :] - m_new).astype(BF16))
        pv = _dot(vt, p)
        if c == 0:
            acc_ref[acc_st, :] = pv
        else:
            acc_ref[acc_st, :] = acc_ref[acc_ld, :] * jnp.exp2(m_old - m_new) + pv
        if c == nk - 1:
            acc = acc_ref[acc_ld, :]
            out = acc[:MLA_V] * (1.0 / acc[MLA_V:MLA_V + 1])
            out = jnp.concatenate([out, jnp.zeros((LANES - MLA_V, tq), F32)], axis=0)
            o_ref[0, t * tq:(t + 1) * tq, :] = out.T.astype(BF16)
        return m_new

    fill(0)
    fill(1)
    m = None
    for v in range(nq * nk):
        if v + 2 < nq * nk:
            fill(v + 2)
        m = update(v, m)


def _mla_attn(qt, k, vt):
    B, H, S, L = k.shape
    tq, tk, nq = MLA_Q_TILE, MLA_KV_TILE, MLA_Q_TILES_PER_STEP
    assert S % tk == 0 and S % (nq * tq) == 0
    return pl.pallas_call(
        functools.partial(_mla_attn_kernel, tq=tq, tk=tk),
        grid=(B, H, S // (nq * tq)),
        in_specs=[
            pl.BlockSpec((1, 1, L, nq * tq), lambda b, h, i: (b, h, 0, i)),
            pl.BlockSpec((1, 1, S, L), lambda b, h, i: (b, h, 0, 0)),
            pl.BlockSpec((1, 1, L, S), lambda b, h, i: (b, h, 0, 0)),
        ],
        out_specs=pl.BlockSpec((1, nq * tq, L), lambda b, h, i: (b, i, h)),
        out_shape=jax.ShapeDtypeStruct((B, S, H * L), BF16),
        scratch_shapes=[pltpu.VMEM((tk + SUBLANES, tq), F32)] * 3
        + [pltpu.VMEM((MLA_VT_ROWS + SUBLANES, tq), F32)] * 2,
        compiler_params=_params(),
        name="mla_attn",
    )(qt, k, vt)


def _pool_mix(up_ref, upp_ref, upn_ref, pw_ref, ps_ref, ext_ref, tm, seq_len):
    i = pl.program_id(1)
    last = pl.num_programs(1) - 1
    ext_ref[0:POOL_HALO] = jnp.where(i > 0, upp_ref[0], 0.0)
    ext_ref[POOL_HALO:POOL_HALO + tm] = up_ref[0]
    ext_ref[POOL_HALO + tm:2 * POOL_HALO + tm] = jnp.where(i < last, upn_ref[0], 0.0)
    t = i * tm + lax.broadcasted_iota(jnp.int32, (tm, 1), 0)
    outs = []
    for g, w in enumerate(POOL_WINDOWS):
        cols = slice(g * POOL_GROUP_W, (g + 1) * POOL_GROUP_W)
        wsum = None
        for o in range(-(w // 2), w - w // 2):
            piece = ext_ref[POOL_HALO + o:POOL_HALO + o + tm, cols]
            wsum = piece if wsum is None else wsum + piece
        lo = jnp.maximum(t - w // 2, 0)
        hi = jnp.minimum(t + (w - 1 - w // 2), seq_len - 1)
        cnt = (hi - lo + 1).astype(F32)
        d = wsum / cnt - ext_ref[POOL_HALO:POOL_HALO + tm, cols]
        outs.append(_dot(d.astype(BF16), pw_ref[g]))
    return jnp.concatenate(outs, axis=-1) * ps_ref[...]


def _mem_xattn(x1s, xg_ref, wq_ref, qg_ref, mk_ref, mv_ref, wo_ref):
    scale = MEM_HEAD_DIM ** -0.5 * LOG2_E
    heads = [slice(hd * MEM_HEAD_DIM, (hd + 1) * MEM_HEAD_DIM) for hd in range(MEM_HEADS)]
    qms = [_dot(_rms(x1, xg_ref[...]).astype(BF16), wq_ref[...]) for x1 in x1s]
    scores = [[_dot_nt((_rms(qm[:, sl], qg_ref[...]) * scale).astype(BF16), mk_ref[0, :, sl]) for sl in heads]
              for qm in qms]
    outs = []
    for x1, per_head in zip(x1s, scores):
        ohs = []
        for s, sl in zip(per_head, heads):
            e = jnp.exp2(s - jnp.max(s, axis=-1, keepdims=True))
            denom = jnp.sum(e, axis=-1, keepdims=True)
            ohs.append((_dot(e.astype(BF16), mv_ref[0, :, sl]) * (1.0 / denom)).astype(BF16))
        outs.append(x1 + _dot(jnp.concatenate(ohs, axis=-1), wo_ref[...]))
    return outs


def _sub_tiles(tm):
    step = tm // OUT_SUB_TILES
    return [slice(r * step, (r + 1) * step) for r in range(OUT_SUB_TILES)]


def _mlp(x2s, fg_ref, w1_ref, w2_ref):
    outs = []
    for x2 in x2s:
        h = _rms(x2, fg_ref[...]).astype(BF16)
        acc = x2
        for c in range(D_FF // FF_CHUNK):
            sl = slice(c * FF_CHUNK, (c + 1) * FF_CHUNK)
            a = jnp.maximum(_dot(h, w1_ref[:, sl]), 0.0)
            acc = acc + _dot((a * a).astype(BF16), w2_ref[sl, :])
        outs.append(acc)
    return outs


def _layer_tail(x1s, tail_refs, out_ref):
    xg_ref, wq_ref, qg_ref, mk_ref, mv_ref, wo_ref, fg_ref, w1_ref, w2_ref = tail_refs
    x2s = _mem_xattn(x1s, xg_ref, wq_ref, qg_ref, mk_ref, mv_ref, wo_ref)
    out_ref[0] = _mlp([jnp.concatenate(x2s, axis=0)], fg_ref, w1_ref, w2_ref)[0]


def _l0_out_kernel(x_ref, up_ref, upp_ref, upn_ref, pw_ref, ps_ref, wa_ref, o_ref, wb_ref, *rest, tm, seq_len):
    *tail_refs, out_ref, ext_ref = rest
    a = _pool_mix(up_ref, upp_ref, upn_ref, pw_ref, ps_ref, ext_ref, tm, seq_len).astype(BF16)
    subs = _sub_tiles(tm)
    x1s = [x_ref[0, rows] + _dot(a[rows], wa_ref[...]) + _dot(o_ref[0, rows], wb_ref[...]) for rows in subs]
    _layer_tail(x1s, tail_refs, out_ref)


def _l1_out_kernel(x_ref, o_ref, wb_ref, *rest, tm):
    *tail_refs, out_ref = rest
    subs = _sub_tiles(tm)
    x1s = [x_ref[0, rows] + _dot(o_ref[0, rows], wb_ref[...]) for rows in subs]
    _layer_tail(x1s, tail_refs, out_ref)


def _tail_specs(n_mem):
    D = D_MODEL
    return [
        _const_spec((1, D)),
        _const_spec((D, D)),
        _const_spec((1, MEM_HEAD_DIM)),
        pl.BlockSpec((1, n_mem, D), lambda b, i: (b, 0, 0), pipeline_mode=pl.Buffered(1)),
        pl.BlockSpec((1, n_mem, D), lambda b, i: (b, 0, 0), pipeline_mode=pl.Buffered(1)),
        _const_spec((D, D)),
        _const_spec((1, D)),
        _const_spec((D, D_FF)),
        _const_spec((D_FF, D)),
    ]


def _l0_out(x, up, pool_w, pool_scale, wa, o, wb, tail):
    B, S, D = x.shape
    n_mem = tail[3].shape[1]
    tm = TOKEN_TILE
    halo_blocks = tm // POOL_HALO
    tile = pl.BlockSpec((1, tm, D), lambda b, i: (b, i, 0))
    return pl.pallas_call(
        functools.partial(_l0_out_kernel, tm=tm, seq_len=S),
        grid=(B, S // tm),
        in_specs=[
            tile,
            pl.BlockSpec((1, tm, POOL_WIDTH), lambda b, i: (b, i, 0)),
            pl.BlockSpec((1, POOL_HALO, POOL_WIDTH),
                         lambda b, i: (b, jnp.maximum(i * halo_blocks - 1, 0), 0)),
            pl.BlockSpec((1, POOL_HALO, POOL_WIDTH),
                         lambda b, i: (b, jnp.minimum((i + 1) * halo_blocks, S // POOL_HALO - 1), 0)),
            _const_spec(pool_w.shape),
            _const_spec((1, POOL_WIDTH)),
            _const_spec((POOL_WIDTH, D)),
            pl.BlockSpec((1, tm, o.shape[-1]), lambda b, i: (b, i, 0)),
            _const_spec(wb.shape),
        ] + _tail_specs(n_mem),
        out_specs=tile,
        out_shape=jax.ShapeDtypeStruct((B, S, D), F32),
        scratch_shapes=[pltpu.VMEM((tm + 2 * POOL_HALO, POOL_WIDTH), F32)],
        compiler_params=_params(),
        name="l0_tail",
    )(x, up, up, up, pool_w, pool_scale, wa, o, wb, *tail)


def _l1_out(x, o, wb, tail):
    B, S, D = x.shape
    n_mem = tail[3].shape[1]
    tm = TOKEN_TILE
    tile = pl.BlockSpec((1, tm, D), lambda b, i: (b, i, 0))
    return pl.pallas_call(
        functools.partial(_l1_out_kernel, tm=tm),
        grid=(B, S // tm),
        in_specs=[tile, pl.BlockSpec((1, tm, o.shape[-1]), lambda b, i: (b, i, 0)), _const_spec(wb.shape)]
        + _tail_specs(n_mem),
        out_specs=tile,
        out_shape=jax.ShapeDtypeStruct((B, S, D), F32),
        compiler_params=_params(),
        name="l1_tail",
    )(x, o, wb, *tail)


def _pair_norm(z, g2, lo_mask):
    y = z * z
    lo = jnp.sum(jnp.where(lo_mask, y, 0.0), axis=-1, keepdims=True)
    hi = jnp.sum(jnp.where(lo_mask, 0.0, y), axis=-1, keepdims=True)
    inv = 1.0 / NA_HEAD_DIM
    r = jnp.where(lo_mask, lax.rsqrt(lo * inv + EPS), lax.rsqrt(hi * inv + EPS))
    return z * r * g2


def _l1_front_kernel(x_ref, g_ref, w_ref, qg_ref, kg_ref, q_out, k_out, v_out):
    h = _rms(x_ref[0], g_ref[...]).astype(BF16)
    qkv = _dot(h, w_ref[...])
    width = NA_HEADS * NA_HEAD_DIM
    lo_mask = lax.broadcasted_iota(jnp.int32, (1, LANES), 1) < NA_HEAD_DIM
    scale = NA_HEAD_DIM ** -0.5 * LOG2_E
    for j in range(width // LANES):
        sl = slice(j * LANES, (j + 1) * LANES)
        q_out[0, :, sl] = (_pair_norm(qkv[:, sl], qg_ref[...], lo_mask) * scale).astype(BF16)
        ksl = slice(width + j * LANES, width + (j + 1) * LANES)
        k_out[0, :, sl] = _pair_norm(qkv[:, ksl], kg_ref[...], lo_mask).astype(BF16)
    v_out[0] = qkv[:, 2 * width:].astype(BF16)


def _l1_front(x, g, w_qkv, qg2, kg2):
    B, S, D = x.shape
    tm = TOKEN_TILE
    tile = pl.BlockSpec((1, tm, D), lambda b, i: (b, i, 0))
    return pl.pallas_call(
        _l1_front_kernel,
        grid=(B, S // tm),
        in_specs=[tile, _const_spec((1, D)), _const_spec((D, 3 * D)), _const_spec((1, LANES)),
                  _const_spec((1, LANES))],
        out_specs=[tile] * 3,
        out_shape=[jax.ShapeDtypeStruct((B, S, D), BF16)] * 3,
        compiler_params=_params(),
        name="l1_front",
    )(x, g, w_qkv, qg2, kg2)


def _na_bias_tiles(rpb_ref, tab_ref):
    c = lax.broadcasted_iota(jnp.int32, (GRID_W, LANES), 0)
    kc = lax.broadcasted_iota(jnp.int32, (GRID_W, LANES), 1) % GRID_W
    c0 = jnp.clip(c - NA_KW // 2, 0, GRID_W - NA_KW)
    valid = (kc >= c0) & (kc < c0 + NA_KW)
    for h in range(tab_ref.shape[0]):
        for dr in range(tab_ref.shape[1]):
            rows = jnp.broadcast_to(rpb_ref[h, dr:dr + 1, :], (GRID_W, LANES))
            skew = pltpu.roll(rows, LANES - (NA_KW - 1), 1, stride=1, stride_axis=0)
            tab_ref[h, dr] = jnp.where(valid, skew * LOG2_E, MASK_VALUE)


def _na_kernel(q_ref, k_ref, v_ref, rpb_ref, o_ref, tab_ref, *s_refs, rows_per_step, n_rows):
    i = pl.program_id(2)
    win = NA_KH * GRID_W
    lo_mask = lax.broadcasted_iota(jnp.int32, (1, LANES), 1) < NA_HEAD_DIM
    zero = jnp.zeros((), BF16)

    @pl.when(i == 0)
    def _():
        _na_bias_tiles(rpb_ref, tab_ref)

    z_st = pl.multiple_of(jnp.minimum(pl.program_id(2), 0) * SUBLANES, SUBLANES)
    z_ld = pl.multiple_of(jnp.minimum(pl.program_id(1), 0) * SUBLANES, SUBLANES)
    st_rows, ld_rows = pl.ds(z_st, 2 * GRID_W), pl.ds(z_ld, 2 * GRID_W)
    row_max = {}
    ones = jnp.ones((win, LANES), BF16)

    def window(rr):
        r = i * rows_per_step + rr
        r0 = jnp.clip(r - NA_KH // 2, 0, n_rows - NA_KH)
        return r - r0, pl.ds(pl.multiple_of(r0 * GRID_W, GRID_W), win)

    def scores(rr):
        delta, keys = window(rr)
        q_row = q_ref[0, rr * GRID_W:(rr + 1) * GRID_W, :]
        q2 = jnp.concatenate([jnp.where(lo_mask, q_row, zero), jnp.where(lo_mask, zero, q_row)], axis=0)
        bias = jnp.concatenate(
            [jnp.concatenate([tab_ref[h, 2 * p + (NA_KH - 1) - delta] for p in range(NA_KH // 2)], axis=1)
             for h in range(2)], axis=0)
        s = _dot_nt(q2, k_ref[0, keys, :]) + bias
        s_refs[rr][st_rows, :] = s
        row_max[rr] = jnp.max(s, axis=-1, keepdims=True)

    def attend(rr):
        _, keys = window(rr)
        p = jnp.exp2((s_refs[rr][ld_rows, :] - row_max.pop(rr)).astype(BF16))
        pv = _dot(p, jnp.concatenate([v_ref[0, keys, :], ones], axis=1))
        o2 = pv[:, :LANES] * (1.0 / pv[:, LANES:])
        out = jnp.where(lo_mask, o2[:GRID_W], o2[GRID_W:])
        o_ref[0, rr * GRID_W:(rr + 1) * GRID_W, :] = out.astype(BF16)

    for rr in range(min(NA_ROWS_AHEAD, rows_per_step)):
        scores(rr)
    for rr in range(rows_per_step):
        if rr + NA_ROWS_AHEAD < rows_per_step:
            scores(rr + NA_ROWS_AHEAD)
        attend(rr)


def _na_attn(q, k, v, rpb_pairs):
    B, S, D = q.shape
    n_rows = S // GRID_W
    tq = NA_ROWS_PER_STEP * GRID_W
    n_dr = rpb_pairs.shape[1]
    kv_spec = pl.BlockSpec((1, S, LANES), lambda b, j, i: (b, 0, j))
    return pl.pallas_call(
        functools.partial(_na_kernel, rows_per_step=NA_ROWS_PER_STEP, n_rows=n_rows),
        grid=(B, D // LANES, S // tq),
        in_specs=[
            pl.BlockSpec((1, tq, LANES), lambda b, j, i: (b, i, j)),
            kv_spec, kv_spec,
            pl.BlockSpec((2, n_dr, LANES), lambda b, j, i: (j, 0, 0)),
        ],
        out_specs=pl.BlockSpec((1, tq, LANES), lambda b, j, i: (b, i, j)),
        out_shape=jax.ShapeDtypeStruct((B, S, D), BF16),
        scratch_shapes=[pltpu.VMEM((2, n_dr, GRID_W, LANES), F32)]
        + [pltpu.VMEM((2 * GRID_W + SUBLANES, NA_KH * GRID_W), F32)] * NA_ROWS_PER_STEP,
        compiler_params=_params(),
        name="na_attn",
    )(q, k, v, rpb_pairs)


def _na_rpb_pairs(rpb):
    padded = jnp.pad(rpb, ((0, 0), (0, 0), (0, GRID_W - rpb.shape[-1])))
    return jnp.concatenate([padded[:, :-1], padded[:, 1:]], axis=-1)


def _pad_heads(w, heads, width):
    lead = w.shape[:-1]
    w = w.reshape(*lead, heads, width)
    w = jnp.pad(w, [(0, 0)] * len(lead) + [(0, 0), (0, LANES - width)])
    return w.reshape(*lead, heads * LANES)


def _rope_tables(seq_len):
    half = MLA_ROPE // 2
    pos = jnp.arange(seq_len, dtype=F32)
    freqs = ROPE_THETA ** (-jnp.arange(half, dtype=F32) / half)
    ang = pos[:, None] * freqs[None, :]
    cos, sin = jnp.cos(ang), jnp.sin(ang)
    z = lambda n: jnp.zeros((seq_len, n), F32)
    cos_t = jnp.concatenate([jnp.ones((seq_len, MLA_NOPE), F32), cos, cos, z(LANES - MLA_QK)], axis=1)
    sin_a = jnp.concatenate([z(MLA_NOPE), -sin, z(half), z(LANES - MLA_QK)], axis=1)
    sin_b = jnp.concatenate([z(MLA_NOPE), z(half), sin, z(LANES - MLA_QK)], axis=1)
    return cos_t, sin_a, sin_b, cos.T, sin.T


def kernel(x, mem, mix_norm_g, xattn_norm_g, ff_norm_g, w_mem_q, mem_q_g, w_mem_o, w_ff1, w_ff2, mem_tok_norm_g, w_mem_kv, mem_k_g, w_in_e, pool_w, pool_scale, q_lora_g, w_uq, kv_lora_g, w_ukv, mla_q_g, mla_k_g, w_out_e, w_qkv_o, na_q_g, na_k_g, na_rpb, w_out_o):
    B, S, D = x.shape
    row = lambda v: v.reshape(1, -1)

    mem_k, mem_v = _mem_kv(mem, row(mem_tok_norm_g), w_mem_kv.astype(BF16), row(mem_k_g))

    o1 = POOL_WIDTH
    o3 = o1 + Q_LORA + KV_LORA
    w_in = w_in_e[0]
    w_in_pad = jnp.concatenate(
        [w_in[:, :o3], jnp.zeros((D, MLA_NOPE), F32), w_in[:, o3:], jnp.zeros((D, LANES - MLA_QK), F32)],
        axis=1).astype(BF16)
    wuqt = _pad_heads(w_uq[0], MLA_HEADS, MLA_QK).T.astype(BF16)
    w_ukv_h = w_ukv[0].reshape(KV_LORA, MLA_HEADS, MLA_NOPE + MLA_V)
    wuk = _pad_heads(w_ukv_h[:, :, :MLA_NOPE].reshape(KV_LORA, -1), MLA_HEADS, MLA_NOPE).astype(BF16)
    wuvt = _pad_heads(w_ukv_h[:, :, MLA_NOPE:].reshape(KV_LORA, -1), MLA_HEADS, MLA_V).T.astype(BF16)
    qg = jnp.pad(mla_q_g[0], (0, LANES - MLA_QK)) * (MLA_QK ** -0.5 * LOG2_E)
    qg = jnp.broadcast_to(qg[:, None], (LANES, LANES))
    kg = jnp.pad(mla_k_g[0], (0, LANES - MLA_QK)).reshape(1, LANES)
    up, q, k, v = _l0_front(x, row(mix_norm_g[0]), w_in_pad, row(q_lora_g[0]), wuqt, row(kv_lora_g[0]),
                            wuk, wuvt, qg, kg, *_rope_tables(S))
    o = _mla_attn(q, k, v)
    w_out = w_out_e[0]
    wa = w_out[:POOL_WIDTH].astype(BF16)
    wb = jnp.pad(w_out[POOL_WIDTH:].reshape(MLA_HEADS, MLA_V, D), ((0, 0), (0, LANES - MLA_V), (0, 0)))
    wb = wb.reshape(MLA_HEADS * LANES, D).astype(BF16)
    def tail(layer):
        return (row(xattn_norm_g[layer]), w_mem_q[layer].astype(BF16), row(mem_q_g[layer]), mem_k, mem_v,
                w_mem_o[layer].astype(BF16), row(ff_norm_g[layer]), w_ff1[layer].astype(BF16),
                w_ff2[layer].astype(BF16))

    x = _l0_out(x, up, pool_w[0].astype(BF16), row(pool_scale[0]), wa, o, wb, tail(0))

    qg2 = jnp.tile(na_q_g[0], LANES // NA_HEAD_DIM).reshape(1, LANES)
    kg2 = jnp.tile(na_k_g[0], LANES // NA_HEAD_DIM).reshape(1, LANES)
    q, k, v = _l1_front(x, row(mix_norm_g[1]), w_qkv_o[0].astype(BF16), qg2, kg2)
    c = _na_attn(q, k, v, _na_rpb_pairs(na_rpb[0]))
    return _l1_out(x, c, w_out_o[0].astype(BF16), tail(1))
```

```python
import functools

import jax
import jax.numpy as jnp
from jax import lax
from jax.experimental import pallas as pl
from jax.experimental.pallas import tpu as pltpu

F32 = jnp.float32
BF16 = jnp.bfloat16

D_MODEL = 1024
GRID_W = 64
EPS = 1e-6
POOL_WIDTH = 512
POOL_GROUP_W = 128
POOL_WINDOWS = (2, 4, 8, 16)
POOL_HALO = 8
MLA_HEADS = 8
MLA_NOPE = 64
MLA_ROPE = 32
MLA_V = 64
MLA_QK = MLA_NOPE + MLA_ROPE
Q_LORA = 256
KV_LORA = 128
ROPE_THETA = 10000.0
NA_HEADS = 16
NA_HEAD_DIM = 64
NA_KH = 8
NA_KW = 16
MEM_HEADS = 4
MEM_HEAD_DIM = 256
D_FF = 4096

LANES = 128
SUBLANES = 8
MASK_VALUE = -1e30
LOG2_E = 1.4426950408889634
VMEM_LIMIT = 56 * 1024 * 1024

TOKEN_TILE = 512
OUT_SUB_TILES = 2
FF_CHUNK = 512
MLA_Q_TILE = 512
MLA_KV_TILE = 512
MLA_Q_TILES_PER_STEP = 4
MLA_VT_ROWS = 80
NA_ROWS_PER_STEP = 32
NA_ROWS_AHEAD = 4


def _const_spec(shape):
    zeros = (0,) * len(shape)
    return pl.BlockSpec(shape, lambda *_: zeros, pipeline_mode=pl.Buffered(1))


def _params():
    return pltpu.CompilerParams(vmem_limit_bytes=VMEM_LIMIT)


def _rms(x, g, n=None):
    n = x.shape[-1] if n is None else n
    ms = jnp.sum(x * x, axis=-1, keepdims=True) * (1.0 / n)
    return x * lax.rsqrt(ms + EPS) * g


def _dot(a, b):
    return jnp.dot(a, b, preferred_element_type=F32)


def _dot_nt(a, b):
    return lax.dot_general(a, b, (((1,), (1,)), ((), ())), preferred_element_type=F32)


def _mem_kv_kernel(mem_ref, g_ref, w_ref, kg_ref, k_out, v_out):
    mn = _rms(mem_ref[0], g_ref[...]).astype(BF16)
    kv = _dot(mn, w_ref[...])
    for h in range(MEM_HEADS):
        sl = slice(h * MEM_HEAD_DIM, (h + 1) * MEM_HEAD_DIM)
        k_out[0, :, sl] = _rms(kv[:, sl], kg_ref[...]).astype(BF16)
    v_out[0] = kv[:, D_MODEL:].astype(BF16)


def _mem_kv(mem, g, w_kv, k_g):
    B, N, D = mem.shape
    return pl.pallas_call(
        _mem_kv_kernel,
        grid=(B,),
        in_specs=[
            pl.BlockSpec((1, N, D), lambda b: (b, 0, 0)),
            _const_spec((1, D)),
            _const_spec((D, 2 * D)),
            _const_spec((1, MEM_HEAD_DIM)),
        ],
        out_specs=[pl.BlockSpec((1, N, D), lambda b: (b, 0, 0))] * 2,
        out_shape=[jax.ShapeDtypeStruct((B, N, D), BF16)] * 2,
        compiler_params=_params(),
        name="mem_kv",
    )(mem, g, w_kv, k_g)


def _l0_front_kernel(x_ref, g_ref, win_ref, qlg_ref, wuqt_ref, kvlg_ref, wuk_ref, wuvt_ref,
                     qg_ref, kg_ref, cos_ref, sa_ref, sb_ref, cost_ref, sint_ref,
                     up_ref, q_ref, k_ref, v_ref):
    subs = _sub_tiles(x_ref.shape[1])
    us = [_dot(_rms(x_ref[0, rows], g_ref[...]).astype(BF16), win_ref[...]) for rows in subs]
    o2 = POOL_WIDTH + Q_LORA
    o3 = o2 + KV_LORA
    half = MLA_ROPE // 2
    r1 = slice(MLA_NOPE, MLA_NOPE + half)
    r2 = slice(MLA_NOPE + half, MLA_QK)
    inv_n = 1.0 / MLA_QK
    for rows, u in zip(subs, us):
        n = rows.stop - rows.start
        up_ref[0, rows] = u[:, :POOL_WIDTH]
        cq = _rms(u[:, POOL_WIDTH:o2], qlg_ref[...])
        ckv = _rms(u[:, o2:o3], kvlg_ref[...])
        kr = u[:, o3:]
        qt = _dot(wuqt_ref[...], cq.T.astype(BF16))
        vt = _dot(wuvt_ref[...], ckv.T.astype(BF16))
        kf = _dot(ckv.astype(BF16), wuk_ref[...])

        cos_t, sin_t = cost_ref[:, rows], sint_ref[:, rows]
        gq = jnp.tile(qg_ref[...], (1, n // LANES))
        row_id = lax.broadcasted_iota(jnp.int32, (LANES, n), 0)
        for hd in range(MLA_HEADS):
            head = slice(hd * LANES, (hd + 1) * LANES)
            blk = qt[head]
            ms = jnp.sum(blk * blk, axis=0, keepdims=True) * inv_n
            y = blk * lax.rsqrt(ms + EPS) * gq
            y1, y2 = y[r1], y[r2]
            q_ref[0, hd, :, rows] = jnp.concatenate(
                [y[:MLA_NOPE], y1 * cos_t - y2 * sin_t, y1 * sin_t + y2 * cos_t, y[MLA_QK:]], axis=0).astype(BF16)
            v_ref[0, hd, :, rows] = jnp.where(row_id == MLA_V, 1.0, vt[head]).astype(BF16)

        krg = kr * kg_ref[...]
        partner = pltpu.roll(krg, LANES - half, 1) * sa_ref[rows] + pltpu.roll(krg, half, 1) * sb_ref[rows]
        gcos = kg_ref[...] * cos_ref[rows]
        for hd in range(MLA_HEADS):
            pre = kf[:, hd * LANES:(hd + 1) * LANES] + kr
            ms = jnp.sum(pre * pre, axis=-1, keepdims=True) * inv_n
            k_ref[0, hd, rows] = ((pre * gcos + partner) * lax.rsqrt(ms + EPS)).astype(BF16)


def _l0_front(x, g, w_in, qlg, wuqt, kvlg, wuk, wuvt, qg, kg, cos, sa, sb, cos_t, sin_t):
    B, S, D = x.shape
    tm = TOKEN_TILE
    hw = MLA_HEADS * LANES
    head_spec = pl.BlockSpec((1, MLA_HEADS, tm, LANES), lambda b, i: (b, 0, i, 0))
    head_t_spec = pl.BlockSpec((1, MLA_HEADS, LANES, tm), lambda b, i: (b, 0, 0, i))
    tab_spec = pl.BlockSpec((tm, LANES), lambda b, i: (i, 0))
    tab_t_spec = pl.BlockSpec((MLA_ROPE // 2, tm), lambda b, i: (0, i))
    return pl.pallas_call(
        _l0_front_kernel,
        grid=(B, S // tm),
        in_specs=[
            pl.BlockSpec((1, tm, D), lambda b, i: (b, i, 0)),
            _const_spec((1, D)),
            _const_spec((D, D)),
            _const_spec((1, Q_LORA)),
            _const_spec((hw, Q_LORA)),
            _const_spec((1, KV_LORA)),
            _const_spec((KV_LORA, hw)),
            _const_spec((hw, KV_LORA)),
            _const_spec((LANES, LANES)),
            _const_spec((1, LANES)),
            tab_spec, tab_spec, tab_spec, tab_t_spec, tab_t_spec,
        ],
        out_specs=[
            pl.BlockSpec((1, tm, POOL_WIDTH), lambda b, i: (b, i, 0)),
            head_t_spec, head_spec, head_t_spec,
        ],
        out_shape=[
            jax.ShapeDtypeStruct((B, S, POOL_WIDTH), F32),
            jax.ShapeDtypeStruct((B, MLA_HEADS, LANES, S), BF16),
            jax.ShapeDtypeStruct((B, MLA_HEADS, S, LANES), BF16),
            jax.ShapeDtypeStruct((B, MLA_HEADS, LANES, S), BF16),
        ],
        compiler_params=_params(),
        name="l0_front",
    )(x, g, w_in, qlg, wuqt, kvlg, wuk, wuvt, qg, kg, cos, sa, sb, cos_t, sin_t)


def _mla_attn_kernel(qt_ref, k_ref, vt_ref, o_ref, s0_ref, s1_ref, s2_ref, acc0_ref, acc1_ref, *, tq, tk):
    nk = k_ref.shape[2] // tk
    nq = qt_ref.shape[3] // tq
    bufs = (s0_ref, s1_ref, s2_ref)
    accs = (acc0_ref, acc1_ref)
    chunk_max = {}
    z_st = pl.multiple_of(jnp.minimum(pl.program_id(2), 0) * SUBLANES, SUBLANES)
    z_ld = pl.multiple_of(jnp.minimum(pl.program_id(1), 0) * SUBLANES, SUBLANES)
    st_rows, ld_rows = pl.ds(z_st, tk), pl.ds(z_ld, tk)
    acc_st, acc_ld = pl.ds(z_st, MLA_VT_ROWS), pl.ds(z_ld, MLA_VT_ROWS)

    def fill(v):
        t, c = divmod(v, nk)
        qt = qt_ref[0, 0, :, t * tq:(t + 1) * tq]
        s = _dot(k_ref[0, 0, c * tk:(c + 1) * tk, :], qt)
        bufs[v % 3][st_rows, :] = s
        chunk_max[v] = jnp.max(s, axis=0, keepdims=True)

    def update(v, m_old):
        t, c = divmod(v, nk)
        s_ref, acc_ref = bufs[v % 3], accs[t % 2]
        vt = vt_ref[0, 0, :MLA_VT_ROWS, c * tk:(c + 1) * tk]
        cm = chunk_max.pop(v)
        m_new = cm if c == 0 else jnp.maximum(m_old, cm)
        p = jnp.exp2((s_ref[ld_rows, :] - m_new).astype(BF16))
        pv = _dot(vt, p)
        if c == 0:
            acc_ref[acc_st, :] = pv
        else:
            acc_ref[acc_st, :] = acc_ref[acc_ld, :] * jnp.exp2(m_old - m_new) + pv
        if c == nk - 1:
            acc = acc_ref[acc_ld, :]
            out = acc[:MLA_V] * (1.0 / acc[MLA_V:MLA_V + 1])
            out = jnp.concatenate([out, jnp.zeros((LANES - MLA_V, tq), F32)], axis=0)
            o_ref[0, t * tq:(t + 1) * tq, :] = out.T.astype(BF16)
        return m_new

    fill(0)
    fill(1)
    m = None
    for v in range(nq * nk):
        if v + 2 < nq * nk:
            fill(v + 2)
        m = update(v, m)


def _mla_attn(qt, k, vt):
    B, H, S, L = k.shape
    tq, tk, nq = MLA_Q_TILE, MLA_KV_TILE, MLA_Q_TILES_PER_STEP
    assert S % tk == 0 and S % (nq * tq) == 0
    return pl.pallas_call(
        functools.partial(_mla_attn_kernel, tq=tq, tk=tk),
        grid=(B, H, S // (nq * tq)),
        in_specs=[
            pl.BlockSpec((1, 1, L, nq * tq), lambda b, h, i: (b, h, 0, i)),
            pl.BlockSpec((1, 1, S, L), lambda b, h, i: (b, h, 0, 0)),
            pl.BlockSpec((1, 1, L, S), lambda b, h, i: (b, h, 0, 0)),
        ],
        out_specs=pl.BlockSpec((1, nq * tq, L), lambda b, h, i: (b, i, h)),
        out_shape=jax.ShapeDtypeStruct((B, S, H * L), BF16),
        scratch_shapes=[pltpu.VMEM((tk + SUBLANES, tq), F32)] * 3
        + [pltpu.VMEM((MLA_VT_ROWS + SUBLANES, tq), F32)] * 2,
        compiler_params=_params(),
        name="mla_attn",
    )(qt, k, vt)


def _pool_mix(up_ref, upp_ref, upn_ref, pw_ref, ps_ref, ext_ref, tm, seq_len):
    i = pl.program_id(1)
    last = pl.num_programs(1) - 1
    ext_ref[0:POOL_HALO] = jnp.where(i > 0, upp_ref[0], 0.0)
    ext_ref[POOL_HALO:POOL_HALO + tm] = up_ref[0]
    ext_ref[POOL_HALO + tm:2 * POOL_HALO + tm] = jnp.where(i < last, upn_ref[0], 0.0)
    t = i * tm + lax.broadcasted_iota(jnp.int32, (tm, 1), 0)
    outs = []
    for g, w in enumerate(POOL_WINDOWS):
        cols = slice(g * POOL_GROUP_W, (g + 1) * POOL_GROUP_W)
        wsum = None
        for o in range(-(w // 2), w - w // 2):
            piece = ext_ref[POOL_HALO + o:POOL_HALO + o + tm, cols]
            wsum = piece if wsum is None else wsum + piece
        lo = jnp.maximum(t - w // 2, 0)
        hi = jnp.minimum(t + (w - 1 - w // 2), seq_len - 1)
        cnt = (hi - lo + 1).astype(F32)
        d = wsum / cnt - ext_ref[POOL_HALO:POOL_HALO + tm, cols]
        outs.append(_dot(d.astype(BF16), pw_ref[g]))
    return jnp.concatenate(outs, axis=-1) * ps_ref[...]


def _mem_xattn(x1s, xg_ref, wq_ref, qg_ref, mk_ref, mv_ref, wo_ref):
    scale = MEM_HEAD_DIM ** -0.5 * LOG2_E
    heads = [slice(hd * MEM_HEAD_DIM, (hd + 1) * MEM_HEAD_DIM) for hd in range(MEM_HEADS)]
    qms = [_dot(_rms(x1, xg_ref[...]).astype(BF16), wq_ref[...]) for x1 in x1s]
    scores = [[_dot_nt((_rms(qm[:, sl], qg_ref[...]) * scale).astype(BF16), mk_ref[0, :, sl]) for sl in heads]
              for qm in qms]
    outs = []
    for x1, per_head in zip(x1s, scores):
        ohs = []
        for s, sl in zip(per_head, heads):
            e = jnp.exp2(s - jnp.max(s, axis=-1, keepdims=True))
            denom = jnp.sum(e, axis=-1, keepdims=True)
            ohs.append((_dot(e.astype(BF16), mv_ref[0, :, sl]) * (1.0 / denom)).astype(BF16))
        outs.append(x1 + _dot(jnp.concatenate(ohs, axis=-1), wo_ref[...]))
    return outs


def _sub_tiles(tm):
    step = tm // OUT_SUB_TILES
    return [slice(r * step, (r + 1) * step) for r in range(OUT_SUB_TILES)]


def _mlp(x2s, fg_ref, w1_ref, w2_ref):
    outs = []
    for x2 in x2s:
        h = _rms(x2, fg_ref[...]).astype(BF16)
        acc = x2
        for c in range(D_FF // FF_CHUNK):
            sl = slice(c * FF_CHUNK, (c + 1) * FF_CHUNK)
            a = jnp.maximum(_dot(h, w1_ref[:, sl]), 0.0)
            acc = acc + _dot((a * a).astype(BF16), w2_ref[sl, :])
        outs.append(acc)
    return outs


def _layer_tail(x1s, tail_refs, out_ref):
    xg_ref, wq_ref, qg_ref, mk_ref, mv_ref, wo_ref, fg_ref, w1_ref, w2_ref = tail_refs
    x2s = _mem_xattn(x1s, xg_ref, wq_ref, qg_ref, mk_ref, mv_ref, wo_ref)
    out_ref[0] = _mlp([jnp.concatenate(x2s, axis=0)], fg_ref, w1_ref, w2_ref)[0]


def _l0_out_kernel(x_ref, up_ref, upp_ref, upn_ref, pw_ref, ps_ref, wa_ref, o_ref, wb_ref, *rest, tm, seq_len):
    *tail_refs, out_ref, ext_ref = rest
    a = _pool_mix(up_ref, upp_ref, upn_ref, pw_ref, ps_ref, ext_ref, tm, seq_len).astype(BF16)
    subs = _sub_tiles(tm)
    x1s = [x_ref[0, rows] + _dot(a[rows], wa_ref[...]) + _dot(o_ref[0, rows], wb_ref[...]) for rows in subs]
    _layer_tail(x1s, tail_refs, out_ref)


def _l1_out_kernel(x_ref, o_ref, wb_ref, *rest, tm):
    *tail_refs, out_ref = rest
    subs = _sub_tiles(tm)
    x1s = [x_ref[0, rows] + _dot(o_ref[0, rows], wb_ref[...]) for rows in subs]
    _layer_tail(x1s, tail_refs, out_ref)


def _tail_specs(n_mem):
    D = D_MODEL
    return [
        _const_spec((1, D)),
        _const_spec((D, D)),
        _const_spec((1, MEM_HEAD_DIM)),
        pl.BlockSpec((1, n_mem, D), lambda b, i: (b, 0, 0), pipeline_mode=pl.Buffered(1)),
        pl.BlockSpec((1, n_mem, D), lambda b, i: (b, 0, 0), pipeline_mode=pl.Buffered(1)),
        _const_spec((D, D)),
        _const_spec((1, D)),
        _const_spec((D, D_FF)),
        _const_spec((D_FF, D)),
    ]


def _l0_out(x, up, pool_w, pool_scale, wa, o, wb, tail):
    B, S, D = x.shape
    n_mem = tail[3].shape[1]
    tm = TOKEN_TILE
    halo_blocks = tm // POOL_HALO
    tile = pl.BlockSpec((1, tm, D), lambda b, i: (b, i, 0))
    return pl.pallas_call(
        functools.partial(_l0_out_kernel, tm=tm, seq_len=S),
        grid=(B, S // tm),
        in_specs=[
            tile,
            pl.BlockSpec((1, tm, POOL_WIDTH), lambda b, i: (b, i, 0)),
            pl.BlockSpec((1, POOL_HALO, POOL_WIDTH),
                         lambda b, i: (b, jnp.maximum(i * halo_blocks - 1, 0), 0)),
            pl.BlockSpec((1, POOL_HALO, POOL_WIDTH),
                         lambda b, i: (b, jnp.minimum((i + 1) * halo_blocks, S // POOL_HALO - 1), 0)),
            _const_spec(pool_w.shape),
            _const_spec((1, POOL_WIDTH)),
            _const_spec((POOL_WIDTH, D)),
            pl.BlockSpec((1, tm, o.shape[-1]), lambda b, i: (b, i, 0)),
            _const_spec(wb.shape),
        ] + _tail_specs(n_mem),
        out_specs=tile,
        out_shape=jax.ShapeDtypeStruct((B, S, D), F32),
        scratch_shapes=[pltpu.VMEM((tm + 2 * POOL_HALO, POOL_WIDTH), F32)],
        compiler_params=_params(),
        name="l0_tail",
    )(x, up, up, up, pool_w, pool_scale, wa, o, wb, *tail)


def _l1_out(x, o, wb, tail):
    B, S, D = x.shape
    n_mem = tail[3].shape[1]
    tm = TOKEN_TILE
    tile = pl.BlockSpec((1, tm, D), lambda b, i: (b, i, 0))
    return pl.pallas_call(
        functools.partial(_l1_out_kernel, tm=tm),
        grid=(B, S // tm),
        in_specs=[tile, pl.BlockSpec((1, tm, o.shape[-1]), lambda b, i: (b, i, 0)), _const_spec(wb.shape)]
        + _tail_specs(n_mem),
        out_specs=tile,
        out_shape=jax.ShapeDtypeStruct((B, S, D), F32),
        compiler_params=_params(),
        name="l1_tail",
    )(x, o, wb, *tail)


def _pair_norm(z, g2, lo_mask):
    y = z * z
    lo = jnp.sum(jnp.where(lo_mask, y, 0.0), axis=-1, keepdims=True)
    hi = jnp.sum(jnp.where(lo_mask, 0.0, y), axis=-1, keepdims=True)
    inv = 1.0 / NA_HEAD_DIM
    r = jnp.where(lo_mask, lax.rsqrt(lo * inv + EPS), lax.rsqrt(hi * inv + EPS))
    return z * r * g2


def _l1_front_kernel(x_ref, g_ref, w_ref, qg_ref, kg_ref, q_out, k_out, v_out):
    subs = _sub_tiles(x_ref.shape[1])
    qkvs = [_dot(_rms(x_ref[0, rows], g_ref[...]).astype(BF16), w_ref[...]) for rows in subs]
    width = NA_HEADS * NA_HEAD_DIM
    lo_mask = lax.broadcasted_iota(jnp.int32, (1, LANES), 1) < NA_HEAD_DIM
    scale = NA_HEAD_DIM ** -0.5 * LOG2_E
    for rows, qkv in zip(subs, qkvs):
        for j in range(width // LANES):
            sl = slice(j * LANES, (j + 1) * LANES)
            q_out[0, rows, sl] = (_pair_norm(qkv[:, sl], qg_ref[...], lo_mask) * scale).astype(BF16)
            ksl = slice(width + j * LANES, width + (j + 1) * LANES)
            k_out[0, rows, sl] = _pair_norm(qkv[:, ksl], kg_ref[...], lo_mask).astype(BF16)
        v_out[0, rows] = qkv[:, 2 * width:].astype(BF16)


def _l1_front(x, g, w_qkv, qg2, kg2):
    B, S, D = x.shape
    tm = TOKEN_TILE
    tile = pl.BlockSpec((1, tm, D), lambda b, i: (b, i, 0))
    return pl.pallas_call(
        _l1_front_kernel,
        grid=(B, S // tm),
        in_specs=[tile, _const_spec((1, D)), _const_spec((D, 3 * D)), _const_spec((1, LANES)),
                  _const_spec((1, LANES))],
        out_specs=[tile] * 3,
        out_shape=[jax.ShapeDtypeStruct((B, S, D), BF16)] * 3,
        compiler_params=_params(),
        name="l1_front",
    )(x, g, w_qkv, qg2, kg2)


def _na_bias_tiles(rpb_ref, tab_ref):
    c = lax.broadcasted_iota(jnp.int32, (GRID_W, LANES), 0)
    kc = lax.broadcasted_iota(jnp.int32, (GRID_W, LANES), 1) % GRID_W
    c0 = jnp.clip(c - NA_KW // 2, 0, GRID_W - NA_KW)
    valid = (kc >= c0) & (kc < c0 + NA_KW)
    for h in range(tab_ref.shape[0]):
        for dr in range(tab_ref.shape[1]):
            rows = jnp.broadcast_to(rpb_ref[h, dr:dr + 1, :], (GRID_W, LANES))
            skew = pltpu.roll(rows, LANES - (NA_KW - 1), 1, stride=1, stride_axis=0)
            tab_ref[h, dr] = jnp.where(valid, skew * LOG2_E, MASK_VALUE)


def _na_kernel(q_ref, k_ref, v_ref, rpb_ref, o_ref, tab_ref, *s_refs, rows_per_step, n_rows):
    i = pl.program_id(2)
    win = NA_KH * GRID_W
    lo_mask = lax.broadcasted_iota(jnp.int32, (1, LANES), 1) < NA_HEAD_DIM
    zero = jnp.zeros((), BF16)

    @pl.when(i == 0)
    def _():
        _na_bias_tiles(rpb_ref, tab_ref)

    z_st = pl.multiple_of(jnp.minimum(pl.program_id(2), 0) * SUBLANES, SUBLANES)
    z_ld = pl.multiple_of(jnp.minimum(pl.program_id(1), 0) * SUBLANES, SUBLANES)
    st_rows, ld_rows = pl.ds(z_st, 2 * GRID_W), pl.ds(z_ld, 2 * GRID_W)
    row_max = {}
    ones = jnp.ones((win, LANES), BF16)

    def window(rr):
        r = i * rows_per_step + rr
        r0 = jnp.clip(r - NA_KH // 2, 0, n_rows - NA_KH)
        return r - r0, pl.ds(pl.multiple_of(r0 * GRID_W, GRID_W), win)

    def scores(rr):
        delta, keys = window(rr)
        q_row = q_ref[0, rr * GRID_W:(rr + 1) * GRID_W, :]
        q2 = jnp.concatenate([jnp.where(lo_mask, q_row, zero), jnp.where(lo_mask, zero, q_row)], axis=0)
        bias = jnp.concatenate(
            [jnp.concatenate([tab_ref[h, 2 * p + (NA_KH - 1) - delta] for p in range(NA_KH // 2)], axis=1)
             for h in range(2)], axis=0)
        s = _dot_nt(q2, k_ref[0, keys, :]) + bias
        s_refs[rr][st_rows, :] = s
        row_max[rr] = jnp.max(s, axis=-1, keepdims=True)

    def attend(rr):
        _, keys = window(rr)
        p = jnp.exp2((s_refs[rr][ld_rows, :] - row_max.pop(rr)).astype(BF16))
        pv = _dot(p, jnp.concatenate([v_ref[0, keys, :], ones], axis=1))
        o2 = pv[:, :LANES] * (1.0 / pv[:, LANES:])
        out = jnp.where(lo_mask, o2[:GRID_W], o2[GRID_W:])
        o_ref[0, rr * GRID_W:(rr + 1) * GRID_W, :] = out.astype(BF16)

    for rr in range(min(NA_ROWS_AHEAD, rows_per_step)):
        scores(rr)
    for rr in range(rows_per_step):
        if rr + NA_ROWS_AHEAD < rows_per_step:
            scores(rr + NA_ROWS_AHEAD)
        attend(rr)


def _na_attn(q, k, v, rpb_pairs):
    B, S, D = q.shape
    n_rows = S // GRID_W
    tq = NA_ROWS_PER_STEP * GRID_W
    n_dr = rpb_pairs.shape[1]
    kv_spec = pl.BlockSpec((1, S, LANES), lambda b, j, i: (b, 0, j))
    return pl.pallas_call(
        functools.partial(_na_kernel, rows_per_step=NA_ROWS_PER_STEP, n_rows=n_rows),
        grid=(B, D // LANES, S // tq),
        in_specs=[
            pl.BlockSpec((1, tq, LANES), lambda b, j, i: (b, i, j)),
            kv_spec, kv_spec,
            pl.BlockSpec((2, n_dr, LANES), lambda b, j, i: (j, 0, 0)),
        ],
        out_specs=pl.BlockSpec((1, tq, LANES), lambda b, j, i: (b, i, j)),
        out_shape=jax.ShapeDtypeStruct((B, S, D), BF16),
        scratch_shapes=[pltpu.VMEM((2, n_dr, GRID_W, LANES), F32)]
        + [pltpu.VMEM((2 * GRID_W + SUBLANES, NA_KH * GRID_W), F32)] * NA_ROWS_PER_STEP,
        compiler_params=_params(),
        name="na_attn",
    )(q, k, v, rpb_pairs)


def _na_rpb_pairs(rpb):
    padded = jnp.pad(rpb, ((0, 0), (0, 0), (0, GRID_W - rpb.shape[-1])))
    return jnp.concatenate([padded[:, :-1], padded[:, 1:]], axis=-1)


def _pad_heads(w, heads, width):
    lead = w.shape[:-1]
    w = w.reshape(*lead, heads, width)
    w = jnp.pad(w, [(0, 0)] * len(lead) + [(0, 0), (0, LANES - width)])
    return w.reshape(*lead, heads * LANES)


def _rope_tables(seq_len):
    half = MLA_ROPE // 2
    pos = jnp.arange(seq_len, dtype=F32)
    freqs = ROPE_THETA ** (-jnp.arange(half, dtype=F32) / half)
    ang = pos[:, None] * freqs[None, :]
    cos, sin = jnp.cos(ang), jnp.sin(ang)
    z = lambda n: jnp.zeros((seq_len, n), F32)
    cos_t = jnp.concatenate([jnp.ones((seq_len, MLA_NOPE), F32), cos, cos, z(LANES - MLA_QK)], axis=1)
    sin_a = jnp.concatenate([z(MLA_NOPE), -sin, z(half), z(LANES - MLA_QK)], axis=1)
    sin_b = jnp.concatenate([z(MLA_NOPE), z(half), sin, z(LANES - MLA_QK)], axis=1)
    return cos_t, sin_a, sin_b, cos.T, sin.T


def kernel(x, mem, mix_norm_g, xattn_norm_g, ff_norm_g, w_mem_q, mem_q_g, w_mem_o, w_ff1, w_ff2, mem_tok_norm_g, w_mem_kv, mem_k_g, w_in_e, pool_w, pool_scale, q_lora_g, w_uq, kv_lora_g, w_ukv, mla_q_g, mla_k_g, w_out_e, w_qkv_o, na_q_g, na_k_g, na_rpb, w_out_o):
    B, S, D = x.shape
    row = lambda v: v.reshape(1, -1)

    mem_k, mem_v = _mem_kv(mem, row(mem_tok_norm_g), w_mem_kv.astype(BF16), row(mem_k_g))

    o1 = POOL_WIDTH
    o3 = o1 + Q_LORA + KV_LORA
    w_in = w_in_e[0]
    w_in_pad = jnp.concatenate(
        [w_in[:, :o3], jnp.zeros((D, MLA_NOPE), F32), w_in[:, o3:], jnp.zeros((D, LANES - MLA_QK), F32)],
        axis=1).astype(BF16)
    wuqt = _pad_heads(w_uq[0], MLA_HEADS, MLA_QK).T.astype(BF16)
    w_ukv_h = w_ukv[0].reshape(KV_LORA, MLA_HEADS, MLA_NOPE + MLA_V)
    wuk = _pad_heads(w_ukv_h[:, :, :MLA_NOPE].reshape(KV_LORA, -1), MLA_HEADS, MLA_NOPE).astype(BF16)
    wuvt = _pad_heads(w_ukv_h[:, :, MLA_NOPE:].reshape(KV_LORA, -1), MLA_HEADS, MLA_V).T.astype(BF16)
    qg = jnp.pad(mla_q_g[0], (0, LANES - MLA_QK)) * (MLA_QK ** -0.5 * LOG2_E)
    qg = jnp.broadcast_to(qg[:, None], (LANES, LANES))
    kg = jnp.pad(mla_k_g[0], (0, LANES - MLA_QK)).reshape(1, LANES)
    up, q, k, v = _l0_front(x, row(mix_norm_g[0]), w_in_pad, row(q_lora_g[0]), wuqt, row(kv_lora_g[0]),
                            wuk, wuvt, qg, kg, *_rope_tables(S))
    o = _mla_attn(q, k, v)
    w_out = w_out_e[0]
    wa = w_out[:POOL_WIDTH].astype(BF16)
    wb = jnp.pad(w_out[POOL_WIDTH:].reshape(MLA_HEADS, MLA_V, D), ((0, 0), (0, LANES - MLA_V), (0, 0)))
    wb = wb.reshape(MLA_HEADS * LANES, D).astype(BF16)
    def tail(layer):
        return (row(xattn_norm_g[layer]), w_mem_q[layer].astype(BF16), row(mem_q_g[layer]), mem_k, mem_v,
                w_mem_o[layer].astype(BF16), row(ff_norm_g[layer]), w_ff1[layer].astype(BF16),
                w_ff2[layer].astype(BF16))

    x = _l0_out(x, up, pool_w[0].astype(BF16), row(pool_scale[0]), wa, o, wb, tail(0))

    qg2 = jnp.tile(na_q_g[0], LANES // NA_HEAD_DIM).reshape(1, LANES)
    kg2 = jnp.tile(na_k_g[0], LANES // NA_HEAD_DIM).reshape(1, LANES)
    q, k, v = _l1_front(x, row(mix_norm_g[1]), w_qkv_o[0].astype(BF16), qg2, kg2)
    c = _na_attn(q, k, v, _na_rpb_pairs(na_rpb[0]))
    return _l1_out(x, c, w_out_o[0].astype(BF16), tail(1))
```

```python
import functools

import jax
import jax.numpy as jnp
from jax import lax
from jax.experimental import pallas as pl
from jax.experimental.pallas import tpu as pltpu

F32 = jnp.float32
BF16 = jnp.bfloat16

D_MODEL = 1024
GRID_W = 64
EPS = 1e-6
POOL_WIDTH = 512
POOL_GROUP_W = 128
POOL_WINDOWS = (2, 4, 8, 16)
POOL_HALO = 8
MLA_HEADS = 8
MLA_NOPE = 64
MLA_ROPE = 32
MLA_V = 64
MLA_QK = MLA_NOPE + MLA_ROPE
Q_LORA = 256
KV_LORA = 128
ROPE_THETA = 10000.0
NA_HEADS = 16
NA_HEAD_DIM = 64
NA_KH = 8
NA_KW = 16
MEM_HEADS = 4
MEM_HEAD_DIM = 256
D_FF = 4096

LANES = 128
SUBLANES = 8
MASK_VALUE = -1e30
LOG2_E = 1.4426950408889634
VMEM_LIMIT = 56 * 1024 * 1024

TOKEN_TILE = 512
OUT_SUB_TILES = 2
FF_CHUNK = 512
MLA_Q_TILE = 512
MLA_KV_TILE = 512
MLA_Q_TILES_PER_STEP = 4
MLA_VT_ROWS = 96
NA_ROWS_PER_STEP = 32
NA_ROWS_AHEAD = 4


def _const_spec(shape):
    zeros = (0,) * len(shape)
    return pl.BlockSpec(shape, lambda *_: zeros, pipeline_mode=pl.Buffered(1))


def _params():
    return pltpu.CompilerParams(vmem_limit_bytes=VMEM_LIMIT)


def _rms(x, g, n=None):
    n = x.shape[-1] if n is None else n
    ms = jnp.sum(x * x, axis=-1, keepdims=True) * (1.0 / n)
    return x * lax.rsqrt(ms + EPS) * g


def _dot(a, b):
    return jnp.dot(a, b, preferred_element_type=F32)


def _dot_nt(a, b):
    return lax.dot_general(a, b, (((1,), (1,)), ((), ())), preferred_element_type=F32)


def _mem_kv_kernel(mem_ref, g_ref, w_ref, kg_ref, k_out, v_out):
    mn = _rms(mem_ref[0], g_ref[...]).astype(BF16)
    kv = _dot(mn, w_ref[...])
    for h in range(MEM_HEADS):
        sl = slice(h * MEM_HEAD_DIM, (h + 1) * MEM_HEAD_DIM)
        k_out[0, :, sl] = _rms(kv[:, sl], kg_ref[...]).astype(BF16)
    v_out[0] = kv[:, D_MODEL:].astype(BF16)


def _mem_kv(mem, g, w_kv, k_g):
    B, N, D = mem.shape
    return pl.pallas_call(
        _mem_kv_kernel,
        grid=(B,),
        in_specs=[
            pl.BlockSpec((1, N, D), lambda b: (b, 0, 0)),
            _const_spec((1, D)),
            _const_spec((D, 2 * D)),
            _const_spec((1, MEM_HEAD_DIM)),
        ],
        out_specs=[pl.BlockSpec((1, N, D), lambda b: (b, 0, 0))] * 2,
        out_shape=[jax.ShapeDtypeStruct((B, N, D), BF16)] * 2,
        compiler_params=_params(),
        name="mem_kv",
    )(mem, g, w_kv, k_g)


def _l0_front_kernel(x_ref, g_ref, win_ref, qlg_ref, wuqt_ref, kvlg_ref, wuk_ref, wuvt_ref,
                     qg_ref, kg_ref, cos_ref, sa_ref, sb_ref, cost_ref, sint_ref,
                     up_ref, q_ref, k_ref, v_ref):
    subs = _sub_tiles(x_ref.shape[1])
    us = [_dot(_rms(x_ref[0, rows], g_ref[...]).astype(BF16), win_ref[...]) for rows in subs]
    o2 = POOL_WIDTH + Q_LORA
    o3 = o2 + KV_LORA
    half = MLA_ROPE // 2
    r1 = slice(MLA_NOPE, MLA_NOPE + half)
    r2 = slice(MLA_NOPE + half, MLA_QK)
    inv_n = 1.0 / MLA_QK
    for rows, u in zip(subs, us):
        n = rows.stop - rows.start
        up_ref[0, rows] = u[:, :POOL_WIDTH]
        cq = _rms(u[:, POOL_WIDTH:o2], qlg_ref[...])
        ckv = _rms(u[:, o2:o3], kvlg_ref[...])
        kr = u[:, o3:]
        qt = _dot(wuqt_ref[...], cq.T.astype(BF16))
        vt = _dot(wuvt_ref[...], ckv.T.astype(BF16))
        kf = _dot(ckv.astype(BF16), wuk_ref[...])

        cos_t, sin_t = cost_ref[:, rows], sint_ref[:, rows]
        gq = jnp.tile(qg_ref[...], (1, n // LANES))
        row_id = lax.broadcasted_iota(jnp.int32, (LANES, n), 0)
        for hd in range(MLA_HEADS):
            head = slice(hd * LANES, (hd + 1) * LANES)
            blk = qt[head]
            ms = jnp.sum(blk * blk, axis=0, keepdims=True) * inv_n
            y = blk * lax.rsqrt(ms + EPS) * gq
            y1, y2 = y[r1], y[r2]
            q_ref[0, hd, :, rows] = jnp.concatenate(
                [y[:MLA_NOPE], y1 * cos_t - y2 * sin_t, y1 * sin_t + y2 * cos_t, y[MLA_QK:]], axis=0).astype(BF16)
            v_ref[0, hd, :, rows] = jnp.where(row_id == MLA_V, 1.0, vt[head]).astype(BF16)

        krg = kr * kg_ref[...]
        partner = pltpu.roll(krg, LANES - half, 1) * sa_ref[rows] + pltpu.roll(krg, half, 1) * sb_ref[rows]
        gcos = kg_ref[...] * cos_ref[rows]
        for hd in range(MLA_HEADS):
            pre = kf[:, hd * LANES:(hd + 1) * LANES] + kr
            ms = jnp.sum(pre * pre, axis=-1, keepdims=True) * inv_n
            k_ref[0, hd, rows] = ((pre * gcos + partner) * lax.rsqrt(ms + EPS)).astype(BF16)


def _l0_front(x, g, w_in, qlg, wuqt, kvlg, wuk, wuvt, qg, kg, cos, sa, sb, cos_t, sin_t):
    B, S, D = x.shape
    tm = TOKEN_TILE
    hw = MLA_HEADS * LANES
    head_spec = pl.BlockSpec((1, MLA_HEADS, tm, LANES), lambda b, i: (b, 0, i, 0))
    head_t_spec = pl.BlockSpec((1, MLA_HEADS, LANES, tm), lambda b, i: (b, 0, 0, i))
    tab_spec = pl.BlockSpec((tm, LANES), lambda b, i: (i, 0))
    tab_t_spec = pl.BlockSpec((MLA_ROPE // 2, tm), lambda b, i: (0, i))
    return pl.pallas_call(
        _l0_front_kernel,
        grid=(B, S // tm),
        in_specs=[
            pl.BlockSpec((1, tm, D), lambda b, i: (b, i, 0)),
            _const_spec((1, D)),
            _const_spec((D, D)),
            _const_spec((1, Q_LORA)),
            _const_spec((hw, Q_LORA)),
            _const_spec((1, KV_LORA)),
            _const_spec((KV_LORA, hw)),
            _const_spec((hw, KV_LORA)),
            _const_spec((LANES, LANES)),
            _const_spec((1, LANES)),
            tab_spec, tab_spec, tab_spec, tab_t_spec, tab_t_spec,
        ],
        out_specs=[
            pl.BlockSpec((1, tm, POOL_WIDTH), lambda b, i: (b, i, 0)),
            head_t_spec, head_spec, head_t_spec,
        ],
        out_shape=[
            jax.ShapeDtypeStruct((B, S, POOL_WIDTH), F32),
            jax.ShapeDtypeStruct((B, MLA_HEADS, LANES, S), BF16),
            jax.ShapeDtypeStruct((B, MLA_HEADS, S, LANES), BF16),
            jax.ShapeDtypeStruct((B, MLA_HEADS, LANES, S), BF16),
        ],
        compiler_params=_params(),
        name="l0_front",
    )(x, g, w_in, qlg, wuqt, kvlg, wuk, wuvt, qg, kg, cos, sa, sb, cos_t, sin_t)


def _mla_attn_kernel(qt_ref, k_ref, vt_ref, o_ref, s0_ref, s1_ref, s2_ref, acc0_ref, acc1_ref, *, tq, tk):
    nk = k_ref.shape[2] // tk
    nq = qt_ref.shape[3] // tq
    bufs = (s0_ref, s1_ref, s2_ref)
    accs = (acc0_ref, acc1_ref)
    chunk_max = {}
    z_st = pl.multiple_of(jnp.minimum(pl.program_id(2), 0) * SUBLANES, SUBLANES)
    z_ld = pl.multiple_of(jnp.minimum(pl.program_id(1), 0) * SUBLANES, SUBLANES)
    st_rows, ld_rows = pl.ds(z_st, tk), pl.ds(z_ld, tk)
    acc_st, acc_ld = pl.ds(z_st, MLA_VT_ROWS), pl.ds(z_ld, MLA_VT_ROWS)

    def fill(v):
        t, c = divmod(v, nk)
        qt = qt_ref[0, 0, :, t * tq:(t + 1) * tq]
        s = _dot(k_ref[0, 0, c * tk:(c + 1) * tk, :], qt)
        bufs[v % 3][st_rows, :] = s
        chunk_max[v] = jnp.max(s, axis=0, keepdims=True)

    def update(v, m_old):
        t, c = divmod(v, nk)
        s_ref, acc_ref = bufs[v % 3], accs[t % 2]
        vt = vt_ref[0, 0, :MLA_VT_ROWS, c * tk:(c + 1) * tk]
        cm = chunk_max.pop(v)
        m_new = cm if c == 0 else jnp.maximum(m_old, cm)
        p = jnp.exp2((s_ref[ld_rows, :] - m_new).astype(BF16))
        pv = _dot(vt, p)
        if c == 0:
            acc_ref[acc_st, :] = pv
        else:
            acc_ref[acc_st, :] = acc_ref[acc_ld, :] * jnp.exp2(m_old - m_new) + pv
        if c == nk - 1:
            acc = acc_ref[acc_ld, :]
            out = acc[:MLA_V] * (1.0 / acc[MLA_V:MLA_V + 1])
            out = jnp.concatenate([out, jnp.zeros((LANES - MLA_V, tq), F32)], axis=0)
            o_ref[0, t * tq:(t + 1) * tq, :] = out.T.astype(BF16)
        return m_new

    fill(0)
    fill(1)
    m = None
    for v in range(nq * nk):
        if v + 2 < nq * nk:
            fill(v + 2)
        m = update(v, m)


def _mla_attn(qt, k, vt):
    B, H, S, L = k.shape
    tq, tk, nq = MLA_Q_TILE, MLA_KV_TILE, MLA_Q_TILES_PER_STEP
    assert S % tk == 0 and S % (nq * tq) == 0
    return pl.pallas_call(
        functools.partial(_mla_attn_kernel, tq=tq, tk=tk),
        grid=(B, H, S // (nq * tq)),
        in_specs=[
            pl.BlockSpec((1, 1, L, nq * tq), lambda b, h, i: (b, h, 0, i)),
            pl.BlockSpec((1, 1, S, L), lambda b, h, i: (b, h, 0, 0)),
            pl.BlockSpec((1, 1, L, S), lambda b, h, i: (b, h, 0, 0)),
        ],
        out_specs=pl.BlockSpec((1, nq * tq, L), lambda b, h, i: (b, i, h)),
        out_shape=jax.ShapeDtypeStruct((B, S, H * L), BF16),
        scratch_shapes=[pltpu.VMEM((tk + SUBLANES, tq), F32)] * 3
        + [pltpu.VMEM((MLA_VT_ROWS + SUBLANES, tq), F32)] * 2,
        compiler_params=_params(),
        name="mla_attn",
    )(qt, k, vt)


def _pool_mix(up_ref, upp_ref, upn_ref, pw_ref, ps_ref, ext_ref, tm, seq_len):
    i = pl.program_id(1)
    last = pl.num_programs(1) - 1
    ext_ref[0:POOL_HALO] = jnp.where(i > 0, upp_ref[0], 0.0)
    ext_ref[POOL_HALO:POOL_HALO + tm] = up_ref[0]
    ext_ref[POOL_HALO + tm:2 * POOL_HALO + tm] = jnp.where(i < last, upn_ref[0], 0.0)
    t = i * tm + lax.broadcasted_iota(jnp.int32, (tm, 1), 0)
    outs = []
    for g, w in enumerate(POOL_WINDOWS):
        cols = slice(g * POOL_GROUP_W, (g + 1) * POOL_GROUP_W)
        wsum = None
        for o in range(-(w // 2), w - w // 2):
            piece = ext_ref[POOL_HALO + o:POOL_HALO + o + tm, cols]
            wsum = piece if wsum is None else wsum + piece
        lo = jnp.maximum(t - w // 2, 0)
        hi = jnp.minimum(t + (w - 1 - w // 2), seq_len - 1)
        cnt = (hi - lo + 1).astype(F32)
        d = wsum / cnt - ext_ref[POOL_HALO:POOL_HALO + tm, cols]
        outs.append(_dot(d.astype(BF16), pw_ref[g]))
    return jnp.concatenate(outs, axis=-1) * ps_ref[...]


def _mem_xattn(x1s, xg_ref, wq_ref, qg_ref, mk_ref, mv_ref, wo_ref):
    scale = MEM_HEAD_DIM ** -0.5 * LOG2_E
    heads = [slice(hd * MEM_HEAD_DIM, (hd + 1) * MEM_HEAD_DIM) for hd in range(MEM_HEADS)]
    qms = [_dot(_rms(x1, xg_ref[...]).astype(BF16), wq_ref[...]) for x1 in x1s]
    scores = [[_dot_nt((_rms(qm[:, sl], qg_ref[...]) * scale).astype(BF16), mk_ref[0, :, sl]) for sl in heads]
              for qm in qms]
    outs = []
    for x1, per_head in zip(x1s, scores):
        ohs = []
        for s, sl in zip(per_head, heads):
            e = jnp.exp2(s - jnp.max(s, axis=-1, keepdims=True))
            denom = jnp.sum(e, axis=-1, keepdims=True)
            ohs.append((_dot(e.astype(BF16), mv_ref[0, :, sl]) * (1.0 / denom)).astype(BF16))
        outs.append(x1 + _dot(jnp.concatenate(ohs, axis=-1), wo_ref[...]))
    return outs


def _sub_tiles(tm):
    step = tm // OUT_SUB_TILES
    return [slice(r * step, (r + 1) * step) for r in range(OUT_SUB_TILES)]


def _mlp(x2s, fg_ref, w1_ref, w2_ref):
    outs = []
    for x2 in x2s:
        h = _rms(x2, fg_ref[...]).astype(BF16)
        acc = x2
        for c in range(D_FF // FF_CHUNK):
            sl = slice(c * FF_CHUNK, (c + 1) * FF_CHUNK)
            a = jnp.maximum(_dot(h, w1_ref[:, sl]), 0.0)
            acc = acc + _dot((a * a).astype(BF16), w2_ref[sl, :])
        outs.append(acc)
    return outs


def _layer_tail(x1s, tail_refs, out_ref):
    xg_ref, wq_ref, qg_ref, mk_ref, mv_ref, wo_ref, fg_ref, w1_ref, w2_ref = tail_refs
    x2s = _mem_xattn(x1s, xg_ref, wq_ref, qg_ref, mk_ref, mv_ref, wo_ref)
    out_ref[0] = _mlp([jnp.concatenate(x2s, axis=0)], fg_ref, w1_ref, w2_ref)[0]


def _l0_out_kernel(x_ref, up_ref, upp_ref, upn_ref, pw_ref, ps_ref, wa_ref, o_ref, wb_ref, *rest, tm, seq_len):
    *tail_refs, out_ref, ext_ref = rest
    a = _pool_mix(up_ref, upp_ref, upn_ref, pw_ref, ps_ref, ext_ref, tm, seq_len).astype(BF16)
    subs = _sub_tiles(tm)
    x1s = [x_ref[0, rows] + _dot(a[rows], wa_ref[...]) + _dot(o_ref[0, rows], wb_ref[...]) for rows in subs]
    _layer_tail(x1s, tail_refs, out_ref)


def _l1_out_kernel(x_ref, o_ref, wb_ref, *rest, tm):
    *tail_refs, out_ref = rest
    subs = _sub_tiles(tm)
    x1s = [x_ref[0, rows] + _dot(o_ref[0, rows], wb_ref[...]) for rows in subs]
    _layer_tail(x1s, tail_refs, out_ref)


def _tail_specs(n_mem):
    D = D_MODEL
    return [
        _const_spec((1, D)),
        _const_spec((D, D)),
        _const_spec((1, MEM_HEAD_DIM)),
        pl.BlockSpec((1, n_mem, D), lambda b, i: (b, 0, 0), pipeline_mode=pl.Buffered(1)),
        pl.BlockSpec((1, n_mem, D), lambda b, i: (b, 0, 0), pipeline_mode=pl.Buffered(1)),
        _const_spec((D, D)),
        _const_spec((1, D)),
        _const_spec((D, D_FF)),
        _const_spec((D_FF, D)),
    ]


def _l0_out(x, up, pool_w, pool_scale, wa, o, wb, tail):
    B, S, D = x.shape
    n_mem = tail[3].shape[1]
    tm = TOKEN_TILE
    halo_blocks = tm // POOL_HALO
    tile = pl.BlockSpec((1, tm, D), lambda b, i: (b, i, 0))
    return pl.pallas_call(
        functools.partial(_l0_out_kernel, tm=tm, seq_len=S),
        grid=(B, S // tm),
        in_specs=[
            tile,
            pl.BlockSpec((1, tm, POOL_WIDTH), lambda b, i: (b, i, 0)),
            pl.BlockSpec((1, POOL_HALO, POOL_WIDTH),
                         lambda b, i: (b, jnp.maximum(i * halo_blocks - 1, 0), 0)),
            pl.BlockSpec((1, POOL_HALO, POOL_WIDTH),
                         lambda b, i: (b, jnp.minimum((i + 1) * halo_blocks, S // POOL_HALO - 1), 0)),
            _const_spec(pool_w.shape),
            _const_spec((1, POOL_WIDTH)),
            _const_spec((POOL_WIDTH, D)),
            pl.BlockSpec((1, tm, o.shape[-1]), lambda b, i: (b, i, 0)),
            _const_spec(wb.shape),
        ] + _tail_specs(n_mem),
        out_specs=tile,
        out_shape=jax.ShapeDtypeStruct((B, S, D), F32),
        scratch_shapes=[pltpu.VMEM((tm + 2 * POOL_HALO, POOL_WIDTH), F32)],
        compiler_params=_params(),
        name="l0_tail",
    )(x, up, up, up, pool_w, pool_scale, wa, o, wb, *tail)


def _l1_out(x, o, wb, tail):
    B, S, D = x.shape
    n_mem = tail[3].shape[1]
    tm = TOKEN_TILE
    tile = pl.BlockSpec((1, tm, D), lambda b, i: (b, i, 0))
    return pl.pallas_call(
        functools.partial(_l1_out_kernel, tm=tm),
        grid=(B, S // tm),
        in_specs=[tile, pl.BlockSpec((1, tm, o.shape[-1]), lambda b, i: (b, i, 0)), _const_spec(wb.shape)]
        + _tail_specs(n_mem),
        out_specs=tile,
        out_shape=jax.ShapeDtypeStruct((B, S, D), F32),
        compiler_params=_params(),
        name="l1_tail",
    )(x, o, wb, *tail)


def _pair_norm(z, g2, lo_mask):
    y = z * z
    lo = jnp.sum(jnp.where(lo_mask, y, 0.0), axis=-1, keepdims=True)
    hi = jnp.sum(jnp.where(lo_mask, 0.0, y), axis=-1, keepdims=True)
    inv = 1.0 / NA_HEAD_DIM
    r = jnp.where(lo_mask, lax.rsqrt(lo * inv + EPS), lax.rsqrt(hi * inv + EPS))
    return z * r * g2


def _l1_front_kernel(x_ref, g_ref, w_ref, qg_ref, kg_ref, q_out, k_out, v_out):
    subs = _sub_tiles(x_ref.shape[1])
    qkvs = [_dot(_rms(x_ref[0, rows], g_ref[...]).astype(BF16), w_ref[...]) for rows in subs]
    width = NA_HEADS * NA_HEAD_DIM
    lo_mask = lax.broadcasted_iota(jnp.int32, (1, LANES), 1) < NA_HEAD_DIM
    scale = NA_HEAD_DIM ** -0.5 * LOG2_E
    for rows, qkv in zip(subs, qkvs):
        for j in range(width // LANES):
            sl = slice(j * LANES, (j + 1) * LANES)
            q_out[0, rows, sl] = (_pair_norm(qkv[:, sl], qg_ref[...], lo_mask) * scale).astype(BF16)
            ksl = slice(width + j * LANES, width + (j + 1) * LANES)
            k_out[0, rows, sl] = _pair_norm(qkv[:, ksl], kg_ref[...], lo_mask).astype(BF16)
        v_out[0, rows] = qkv[:, 2 * width:].astype(BF16)


def _l1_front(x, g, w_qkv, qg2, kg2):
    B, S, D = x.shape
    tm = TOKEN_TILE
    tile = pl.BlockSpec((1, tm, D), lambda b, i: (b, i, 0))
    return pl.pallas_call(
        _l1_front_kernel,
        grid=(B, S // tm),
        in_specs=[tile, _const_spec((1, D)), _const_spec((D, 3 * D)), _const_spec((1, LANES)),
                  _const_spec((1, LANES))],
        out_specs=[tile] * 3,
        out_shape=[jax.ShapeDtypeStruct((B, S, D), BF16)] * 3,
        compiler_params=_params(),
        name="l1_front",
    )(x, g, w_qkv, qg2, kg2)


def _na_bias_tiles(rpb_ref, tab_ref):
    c = lax.broadcasted_iota(jnp.int32, (GRID_W, LANES), 0)
    kc = lax.broadcasted_iota(jnp.int32, (GRID_W, LANES), 1) % GRID_W
    c0 = jnp.clip(c - NA_KW // 2, 0, GRID_W - NA_KW)
    valid = (kc >= c0) & (kc < c0 + NA_KW)
    for h in range(tab_ref.shape[0]):
        for dr in range(tab_ref.shape[1]):
            rows = jnp.broadcast_to(rpb_ref[h, dr:dr + 1, :], (GRID_W, LANES))
            skew = pltpu.roll(rows, LANES - (NA_KW - 1), 1, stride=1, stride_axis=0)
            tab_ref[h, dr] = jnp.where(valid, skew * LOG2_E, MASK_VALUE)


def _na_kernel(q_ref, k_ref, v_ref, rpb_ref, o_ref, tab_ref, *s_refs, rows_per_step, n_rows):
    i = pl.program_id(2)
    win = NA_KH * GRID_W
    lo_mask = lax.broadcasted_iota(jnp.int32, (1, LANES), 1) < NA_HEAD_DIM
    zero = jnp.zeros((), BF16)

    @pl.when(i == 0)
    def _():
        _na_bias_tiles(rpb_ref, tab_ref)

    z_st = pl.multiple_of(jnp.minimum(pl.program_id(2), 0) * SUBLANES, SUBLANES)
    z_ld = pl.multiple_of(jnp.minimum(pl.program_id(1), 0) * SUBLANES, SUBLANES)
    st_rows, ld_rows = pl.ds(z_st, 2 * GRID_W), pl.ds(z_ld, 2 * GRID_W)
    row_max = {}
    ones = jnp.ones((win, LANES), BF16)

    def window(rr):
        r = i * rows_per_step + rr
        r0 = jnp.clip(r - NA_KH // 2, 0, n_rows - NA_KH)
        return r - r0, pl.ds(pl.multiple_of(r0 * GRID_W, GRID_W), win)

    def scores(rr):
        delta, keys = window(rr)
        q_row = q_ref[0, rr * GRID_W:(rr + 1) * GRID_W, :]
        q2 = jnp.concatenate([jnp.where(lo_mask, q_row, zero), jnp.where(lo_mask, zero, q_row)], axis=0)
        bias = jnp.concatenate(
            [jnp.concatenate([tab_ref[h, 2 * p + (NA_KH - 1) - delta] for p in range(NA_KH // 2)], axis=1)
             for h in range(2)], axis=0)
        s = _dot_nt(q2, k_ref[0, keys, :]) + bias
        s_refs[rr][st_rows, :] = s
        row_max[rr] = jnp.max(s, axis=-1, keepdims=True)

    def attend(rr):
        _, keys = window(rr)
        p = jnp.exp2((s_refs[rr][ld_rows, :] - row_max.pop(rr)).astype(BF16))
        pv = _dot(p, jnp.concatenate([v_ref[0, keys, :], ones], axis=1))
        o2 = pv[:, :LANES] * (1.0 / pv[:, LANES:])
        out = jnp.where(lo_mask, o2[:GRID_W], o2[GRID_W:])
        o_ref[0, rr * GRID_W:(rr + 1) * GRID_W, :] = out.astype(BF16)

    for rr in range(min(NA_ROWS_AHEAD, rows_per_step)):
        scores(rr)
    for rr in range(rows_per_step):
        if rr + NA_ROWS_AHEAD < rows_per_step:
            scores(rr + NA_ROWS_AHEAD)
        attend(rr)


def _na_attn(q, k, v, rpb_pairs):
    B, S, D = q.shape
    n_rows = S // GRID_W
    tq = NA_ROWS_PER_STEP * GRID_W
    n_dr = rpb_pairs.shape[1]
    kv_spec = pl.BlockSpec((1, S, LANES), lambda b, j, i: (b, 0, j))
    return pl.pallas_call(
        functools.partial(_na_kernel, rows_per_step=NA_ROWS_PER_STEP, n_rows=n_rows),
        grid=(B, D // LANES, S // tq),
        in_specs=[
            pl.BlockSpec((1, tq, LANES), lambda b, j, i: (b, i, j)),
            kv_spec, kv_spec,
            pl.BlockSpec((2, n_dr, LANES), lambda b, j, i: (j, 0, 0)),
        ],
        out_specs=pl.BlockSpec((1, tq, LANES), lambda b, j, i: (b, i, j)),
        out_shape=jax.ShapeDtypeStruct((B, S, D), BF16),
        scratch_shapes=[pltpu.VMEM((2, n_dr, GRID_W, LANES), F32)]
        + [pltpu.VMEM((2 * GRID_W + SUBLANES, NA_KH * GRID_W), F32)] * NA_ROWS_PER_STEP,
        compiler_params=_params(),
        name="na_attn",
    )(q, k, v, rpb_pairs)


def _na_rpb_pairs(rpb):
    padded = jnp.pad(rpb, ((0, 0), (0, 0), (0, GRID_W - rpb.shape[-1])))
    return jnp.concatenate([padded[:, :-1], padded[:, 1:]], axis=-1)


def _pad_heads(w, heads, width):
    lead = w.shape[:-1]
    w = w.reshape(*lead, heads, width)
    w = jnp.pad(w, [(0, 0)] * len(lead) + [(0, 0), (0, LANES - width)])
    return w.reshape(*lead, heads * LANES)


def _rope_tables(seq_len):
    half = MLA_ROPE // 2
    pos = jnp.arange(seq_len, dtype=F32)
    freqs = ROPE_THETA ** (-jnp.arange(half, dtype=F32) / half)
    ang = pos[:, None] * freqs[None, :]
    cos, sin = jnp.cos(ang), jnp.sin(ang)
    z = lambda n: jnp.zeros((seq_len, n), F32)
    cos_t = jnp.concatenate([jnp.ones((seq_len, MLA_NOPE), F32), cos, cos, z(LANES - MLA_QK)], axis=1)
    sin_a = jnp.concatenate([z(MLA_NOPE), -sin, z(half), z(LANES - MLA_QK)], axis=1)
    sin_b = jnp.concatenate([z(MLA_NOPE), z(half), sin, z(LANES - MLA_QK)], axis=1)
    return cos_t, sin_a, sin_b, cos.T, sin.T


def kernel(x, mem, mix_norm_g, xattn_norm_g, ff_norm_g, w_mem_q, mem_q_g, w_mem_o, w_ff1, w_ff2, mem_tok_norm_g, w_mem_kv, mem_k_g, w_in_e, pool_w, pool_scale, q_lora_g, w_uq, kv_lora_g, w_ukv, mla_q_g, mla_k_g, w_out_e, w_qkv_o, na_q_g, na_k_g, na_rpb, w_out_o):
    B, S, D = x.shape
    row = lambda v: v.reshape(1, -1)

    mem_k, mem_v = _mem_kv(mem, row(mem_tok_norm_g), w_mem_kv.astype(BF16), row(mem_k_g))

    o1 = POOL_WIDTH
    o3 = o1 + Q_LORA + KV_LORA
    w_in = w_in_e[0]
    w_in_pad = jnp.concatenate(
        [w_in[:, :o3], jnp.zeros((D, MLA_NOPE), F32), w_in[:, o3:], jnp.zeros((D, LANES - MLA_QK), F32)],
        axis=1).astype(BF16)
    wuqt = _pad_heads(w_uq[0], MLA_HEADS, MLA_QK).T.astype(BF16)
    w_ukv_h = w_ukv[0].reshape(KV_LORA, MLA_HEADS, MLA_NOPE + MLA_V)
    wuk = _pad_heads(w_ukv_h[:, :, :MLA_NOPE].reshape(KV_LORA, -1), MLA_HEADS, MLA_NOPE).astype(BF16)
    wuvt = _pad_heads(w_ukv_h[:, :, MLA_NOPE:].reshape(KV_LORA, -1), MLA_HEADS, MLA_V).T.astype(BF16)
    qg = jnp.pad(mla_q_g[0], (0, LANES - MLA_QK)) * (MLA_QK ** -0.5 * LOG2_E)
    qg = jnp.broadcast_to(qg[:, None], (LANES, LANES))
    kg = jnp.pad(mla_k_g[0], (0, LANES - MLA_QK)).reshape(1, LANES)
    up, q, k, v = _l0_front(x, row(mix_norm_g[0]), w_in_pad, row(q_lora_g[0]), wuqt, row(kv_lora_g[0]),
                            wuk, wuvt, qg, kg, *_rope_tables(S))
    o = _mla_attn(q, k, v)
    w_out = w_out_e[0]
    wa = w_out[:POOL_WIDTH].astype(BF16)
    wb = jnp.pad(w_out[POOL_WIDTH:].reshape(MLA_HEADS, MLA_V, D), ((0, 0), (0, LANES - MLA_V), (0, 0)))
    wb = wb.reshape(MLA_HEADS * LANES, D).astype(BF16)
    def tail(layer):
        return (row(xattn_norm_g[layer]), w_mem_q[layer].astype(BF16), row(mem_q_g[layer]), mem_k, mem_v,
                w_mem_o[layer].astype(BF16), row(ff_norm_g[layer]), w_ff1[layer].astype(BF16),
                w_ff2[layer].astype(BF16))

    x = _l0_out(x, up, pool_w[0].astype(BF16), row(pool_scale[0]), wa, o, wb, tail(0))

    qg2 = jnp.tile(na_q_g[0], LANES // NA_HEAD_DIM).reshape(1, LANES)
    kg2 = jnp.tile(na_k_g[0], LANES // NA_HEAD_DIM).reshape(1, LANES)
    q, k, v = _l1_front(x, row(mix_norm_g[1]), w_qkv_o[0].astype(BF16), qg2, kg2)
    c = _na_attn(q, k, v, _na_rpb_pairs(na_rpb[0]))
    return _l1_out(x, c, w_out_o[0].astype(BF16), tail(1))
```

```python
import functools

import jax
import jax.numpy as jnp
from jax import lax
from jax.experimental import pallas as pl
from jax.experimental.pallas import tpu as pltpu

F32 = jnp.float32
BF16 = jnp.bfloat16

D_MODEL = 1024
GRID_W = 64
EPS = 1e-6
POOL_WIDTH = 512
POOL_GROUP_W = 128
POOL_WINDOWS = (2, 4, 8, 16)
POOL_HALO = 8
MLA_HEADS = 8
MLA_NOPE = 64
MLA_ROPE = 32
MLA_V = 64
MLA_QK = MLA_NOPE + MLA_ROPE
Q_LORA = 256
KV_LORA = 128
ROPE_THETA = 10000.0
NA_HEADS = 16
NA_HEAD_DIM = 64
NA_KH = 8
NA_KW = 16
MEM_HEADS = 4
MEM_HEAD_DIM = 256
D_FF = 4096

LANES = 128
SUBLANES = 8
MASK_VALUE = -1e30
LOG2_E = 1.4426950408889634
VMEM_LIMIT = 56 * 1024 * 1024

TOKEN_TILE = 512
OUT_SUB_TILES = 2
FF_CHUNK = 512
MLA_Q_TILE = 512
MLA_KV_TILE = 512
MLA_Q_TILES_PER_STEP = 4
MLA_VT_ROWS = 96
NA_ROWS_PER_STEP = 64
NA_ROWS_AHEAD = 4


def _const_spec(shape):
    zeros = (0,) * len(shape)
    return pl.BlockSpec(shape, lambda *_: zeros, pipeline_mode=pl.Buffered(1))


def _params():
    return pltpu.CompilerParams(vmem_limit_bytes=VMEM_LIMIT)


def _rms(x, g, n=None):
    n = x.shape[-1] if n is None else n
    ms = jnp.sum(x * x, axis=-1, keepdims=True) * (1.0 / n)
    return x * lax.rsqrt(ms + EPS) * g


def _dot(a, b):
    return jnp.dot(a, b, preferred_element_type=F32)


def _dot_nt(a, b):
    return lax.dot_general(a, b, (((1,), (1,)), ((), ())), preferred_element_type=F32)


def _mem_kv_kernel(mem_ref, g_ref, w_ref, kg_ref, k_out, v_out):
    mn = _rms(mem_ref[0], g_ref[...]).astype(BF16)
    kv = _dot(mn, w_ref[...])
    for h in range(MEM_HEADS):
        sl = slice(h * MEM_HEAD_DIM, (h + 1) * MEM_HEAD_DIM)
        k_out[0, :, sl] = _rms(kv[:, sl], kg_ref[...]).astype(BF16)
    v_out[0] = kv[:, D_MODEL:].astype(BF16)


def _mem_kv(mem, g, w_kv, k_g):
    B, N, D = mem.shape
    return pl.pallas_call(
        _mem_kv_kernel,
        grid=(B,),
        in_specs=[
            pl.BlockSpec((1, N, D), lambda b: (b, 0, 0)),
            _const_spec((1, D)),
            _const_spec((D, 2 * D)),
            _const_spec((1, MEM_HEAD_DIM)),
        ],
        out_specs=[pl.BlockSpec((1, N, D), lambda b: (b, 0, 0))] * 2,
        out_shape=[jax.ShapeDtypeStruct((B, N, D), BF16)] * 2,
        compiler_params=_params(),
        name="mem_kv",
    )(mem, g, w_kv, k_g)


def _l0_front_kernel(x_ref, g_ref, win_ref, qlg_ref, wuqt_ref, kvlg_ref, wuk_ref, wuvt_ref,
                     qg_ref, kg_ref, cos_ref, sa_ref, sb_ref, cost_ref, sint_ref,
                     up_ref, q_ref, k_ref, v_ref):
    subs = _sub_tiles(x_ref.shape[1])
    us = [_dot(_rms(x_ref[0, rows], g_ref[...]).astype(BF16), win_ref[...]) for rows in subs]
    o2 = POOL_WIDTH + Q_LORA
    o3 = o2 + KV_LORA
    half = MLA_ROPE // 2
    r1 = slice(MLA_NOPE, MLA_NOPE + half)
    r2 = slice(MLA_NOPE + half, MLA_QK)
    inv_n = 1.0 / MLA_QK
    for rows, u in zip(subs, us):
        n = rows.stop - rows.start
        up_ref[0, rows] = u[:, :POOL_WIDTH]
        cq = _rms(u[:, POOL_WIDTH:o2], qlg_ref[...])
        ckv = _rms(u[:, o2:o3], kvlg_ref[...])
        kr = u[:, o3:]
        qt = _dot(wuqt_ref[...], cq.T.astype(BF16))
        vt = _dot(wuvt_ref[...], ckv.T.astype(BF16))
        kf = _dot(ckv.astype(BF16), wuk_ref[...])

        cos_t, sin_t = cost_ref[:, rows], sint_ref[:, rows]
        gq = jnp.tile(qg_ref[...], (1, n // LANES))
        row_id = lax.broadcasted_iota(jnp.int32, (LANES, n), 0)
        for hd in range(MLA_HEADS):
            head = slice(hd * LANES, (hd + 1) * LANES)
            blk = qt[head]
            ms = jnp.sum(blk * blk, axis=0, keepdims=True) * inv_n
            y = blk * lax.rsqrt(ms + EPS) * gq
            y1, y2 = y[r1], y[r2]
            q_ref[0, hd, :, rows] = jnp.concatenate(
                [y[:MLA_NOPE], y1 * cos_t - y2 * sin_t, y1 * sin_t + y2 * cos_t, y[MLA_QK:]], axis=0).astype(BF16)
            v_ref[0, hd, :, rows] = jnp.where(row_id == MLA_V, 1.0, vt[head]).astype(BF16)

        krg = kr * kg_ref[...]
        partner = pltpu.roll(krg, LANES - half, 1) * sa_ref[rows] + pltpu.roll(krg, half, 1) * sb_ref[rows]
        gcos = kg_ref[...] * cos_ref[rows]
        for hd in range(MLA_HEADS):
            pre = kf[:, hd * LANES:(hd + 1) * LANES] + kr
            ms = jnp.sum(pre * pre, axis=-1, keepdims=True) * inv_n
            k_ref[0, hd, rows] = ((pre * gcos + partner) * lax.rsqrt(ms + EPS)).astype(BF16)


def _l0_front(x, g, w_in, qlg, wuqt, kvlg, wuk, wuvt, qg, kg, cos, sa, sb, cos_t, sin_t):
    B, S, D = x.shape
    tm = TOKEN_TILE
    hw = MLA_HEADS * LANES
    head_spec = pl.BlockSpec((1, MLA_HEADS, tm, LANES), lambda b, i: (b, 0, i, 0))
    head_t_spec = pl.BlockSpec((1, MLA_HEADS, LANES, tm), lambda b, i: (b, 0, 0, i))
    tab_spec = pl.BlockSpec((tm, LANES), lambda b, i: (i, 0))
    tab_t_spec = pl.BlockSpec((MLA_ROPE // 2, tm), lambda b, i: (0, i))
    return pl.pallas_call(
        _l0_front_kernel,
        grid=(B, S // tm),
        in_specs=[
            pl.BlockSpec((1, tm, D), lambda b, i: (b, i, 0)),
            _const_spec((1, D)),
            _const_spec((D, D)),
            _const_spec((1, Q_LORA)),
            _const_spec((hw, Q_LORA)),
            _const_spec((1, KV_LORA)),
            _const_spec((KV_LORA, hw)),
            _const_spec((hw, KV_LORA)),
            _const_spec((LANES, LANES)),
            _const_spec((1, LANES)),
            tab_spec, tab_spec, tab_spec, tab_t_spec, tab_t_spec,
        ],
        out_specs=[
            pl.BlockSpec((1, tm, POOL_WIDTH), lambda b, i: (b, i, 0)),
            head_t_spec, head_spec, head_t_spec,
        ],
        out_shape=[
            jax.ShapeDtypeStruct((B, S, POOL_WIDTH), F32),
            jax.ShapeDtypeStruct((B, MLA_HEADS, LANES, S), BF16),
            jax.ShapeDtypeStruct((B, MLA_HEADS, S, LANES), BF16),
            jax.ShapeDtypeStruct((B, MLA_HEADS, LANES, S), BF16),
        ],
        compiler_params=_params(),
        name="l0_front",
    )(x, g, w_in, qlg, wuqt, kvlg, wuk, wuvt, qg, kg, cos, sa, sb, cos_t, sin_t)


def _mla_attn_kernel(qt_ref, k_ref, vt_ref, o_ref, s0_ref, s1_ref, s2_ref, acc0_ref, acc1_ref, *, tq, tk):
    nk = k_ref.shape[2] // tk
    nq = qt_ref.shape[3] // tq
    bufs = (s0_ref, s1_ref, s2_ref)
    accs = (acc0_ref, acc1_ref)
    chunk_max = {}
    z_st = pl.multiple_of(jnp.minimum(pl.program_id(2), 0) * SUBLANES, SUBLANES)
    z_ld = pl.multiple_of(jnp.minimum(pl.program_id(1), 0) * SUBLANES, SUBLANES)
    st_rows, ld_rows = pl.ds(z_st, tk), pl.ds(z_ld, tk)
    acc_st, acc_ld = pl.ds(z_st, MLA_VT_ROWS), pl.ds(z_ld, MLA_VT_ROWS)

    def fill(v):
        t, c = divmod(v, nk)
        qt = qt_ref[0, 0, :, t * tq:(t + 1) * tq]
        s = _dot(k_ref[0, 0, c * tk:(c + 1) * tk, :], qt)
        bufs[v % 3][st_rows, :] = s
        chunk_max[v] = jnp.max(s, axis=0, keepdims=True)

    def update(v, m_old):
        t, c = divmod(v, nk)
        s_ref, acc_ref = bufs[v % 3], accs[t % 2]
        vt = vt_ref[0, 0, :MLA_VT_ROWS, c * tk:(c + 1) * tk]
        cm = chunk_max.pop(v)
        m_new = cm if c == 0 else jnp.maximum(m_old, cm)
        p = jnp.exp2((s_ref[ld_rows, :] - m_new).astype(BF16))
        pv = _dot(vt, p)
        if c == 0:
            acc_ref[acc_st, :] = pv
        else:
            acc_ref[acc_st, :] = acc_ref[acc_ld, :] * jnp.exp2(m_old - m_new) + pv
        if c == nk - 1:
            acc = acc_ref[acc_ld, :]
            out = acc[:MLA_V] * (1.0 / acc[MLA_V:MLA_V + 1])
            out = jnp.concatenate([out, jnp.zeros((LANES - MLA_V, tq), F32)], axis=0)
            o_ref[0, t * tq:(t + 1) * tq, :] = out.T.astype(BF16)
        return m_new

    fill(0)
    fill(1)
    m = None
    for v in range(nq * nk):
        if v + 2 < nq * nk:
            fill(v + 2)
        m = update(v, m)


def _mla_attn(qt, k, vt):
    B, H, S, L = k.shape
    tq, tk, nq = MLA_Q_TILE, MLA_KV_TILE, MLA_Q_TILES_PER_STEP
    assert S % tk == 0 and S % (nq * tq) == 0
    return pl.pallas_call(
        functools.partial(_mla_attn_kernel, tq=tq, tk=tk),
        grid=(B, H, S // (nq * tq)),
        in_specs=[
            pl.BlockSpec((1, 1, L, nq * tq), lambda b, h, i: (b, h, 0, i)),
            pl.BlockSpec((1, 1, S, L), lambda b, h, i: (b, h, 0, 0)),
            pl.BlockSpec((1, 1, L, S), lambda b, h, i: (b, h, 0, 0)),
        ],
        out_specs=pl.BlockSpec((1, nq * tq, L), lambda b, h, i: (b, i, h)),
        out_shape=jax.ShapeDtypeStruct((B, S, H * L), BF16),
        scratch_shapes=[pltpu.VMEM((tk + SUBLANES, tq), F32)] * 3
        + [pltpu.VMEM((MLA_VT_ROWS + SUBLANES, tq), F32)] * 2,
        compiler_params=_params(),
        name="mla_attn",
    )(qt, k, vt)


def _pool_mix(up_ref, upp_ref, upn_ref, pw_ref, ps_ref, ext_ref, tm, seq_len):
    i = pl.program_id(1)
    last = pl.num_programs(1) - 1
    ext_ref[0:POOL_HALO] = jnp.where(i > 0, upp_ref[0], 0.0)
    ext_ref[POOL_HALO:POOL_HALO + tm] = up_ref[0]
    ext_ref[POOL_HALO + tm:2 * POOL_HALO + tm] = jnp.where(i < last, upn_ref[0], 0.0)
    t = i * tm + lax.broadcasted_iota(jnp.int32, (tm, 1), 0)
    outs = []
    for g, w in enumerate(POOL_WINDOWS):
        cols = slice(g * POOL_GROUP_W, (g + 1) * POOL_GROUP_W)
        wsum = None
        for o in range(-(w // 2), w - w // 2):
            piece = ext_ref[POOL_HALO + o:POOL_HALO + o + tm, cols]
            wsum = piece if wsum is None else wsum + piece
        lo = jnp.maximum(t - w // 2, 0)
        hi = jnp.minimum(t + (w - 1 - w // 2), seq_len - 1)
        cnt = (hi - lo + 1).astype(F32)
        d = wsum / cnt - ext_ref[POOL_HALO:POOL_HALO + tm, cols]
        outs.append(_dot(d.astype(BF16), pw_ref[g]))
    return jnp.concatenate(outs, axis=-1) * ps_ref[...]


def _mem_xattn(x1s, xg_ref, wq_ref, qg_ref, mk_ref, mv_ref, wo_ref):
    scale = MEM_HEAD_DIM ** -0.5 * LOG2_E
    heads = [slice(hd * MEM_HEAD_DIM, (hd + 1) * MEM_HEAD_DIM) for hd in range(MEM_HEADS)]
    qms = [_dot(_rms(x1, xg_ref[...]).astype(BF16), wq_ref[...]) for x1 in x1s]
    scores = [[_dot_nt((_rms(qm[:, sl], qg_ref[...]) * scale).astype(BF16), mk_ref[0, :, sl]) for sl in heads]
              for qm in qms]
    outs = []
    for x1, per_head in zip(x1s, scores):
        ohs = []
        for s, sl in zip(per_head, heads):
            e = jnp.exp2(s - jnp.max(s, axis=-1, keepdims=True))
            denom = jnp.sum(e, axis=-1, keepdims=True)
            ohs.append((_dot(e.astype(BF16), mv_ref[0, :, sl]) * (1.0 / denom)).astype(BF16))
        outs.append(x1 + _dot(jnp.concatenate(ohs, axis=-1), wo_ref[...]))
    return outs


def _sub_tiles(tm):
    step = tm // OUT_SUB_TILES
    return [slice(r * step, (r + 1) * step) for r in range(OUT_SUB_TILES)]


def _mlp(x2s, fg_ref, w1_ref, w2_ref):
    outs = []
    for x2 in x2s:
        h = _rms(x2, fg_ref[...]).astype(BF16)
        acc = x2
        for c in range(D_FF // FF_CHUNK):
            sl = slice(c * FF_CHUNK, (c + 1) * FF_CHUNK)
            a = jnp.maximum(_dot(h, w1_ref[:, sl]), 0.0)
            acc = acc + _dot((a * a).astype(BF16), w2_ref[sl, :])
        outs.append(acc)
    return outs


def _layer_tail(x1s, tail_refs, out_ref):
    xg_ref, wq_ref, qg_ref, mk_ref, mv_ref, wo_ref, fg_ref, w1_ref, w2_ref = tail_refs
    x2s = _mem_xattn(x1s, xg_ref, wq_ref, qg_ref, mk_ref, mv_ref, wo_ref)
    out_ref[0] = _mlp([jnp.concatenate(x2s, axis=0)], fg_ref, w1_ref, w2_ref)[0]


def _l0_out_kernel(x_ref, up_ref, upp_ref, upn_ref, pw_ref, ps_ref, wa_ref, o_ref, wb_ref, *rest, tm, seq_len):
    *tail_refs, out_ref, ext_ref = rest
    a = _pool_mix(up_ref, upp_ref, upn_ref, pw_ref, ps_ref, ext_ref, tm, seq_len).astype(BF16)
    subs = _sub_tiles(tm)
    x1s = [x_ref[0, rows] + _dot(a[rows], wa_ref[...]) + _dot(o_ref[0, rows], wb_ref[...]) for rows in subs]
    _layer_tail(x1s, tail_refs, out_ref)


def _l1_out_kernel(x_ref, o_ref, wb_ref, *rest, tm):
    *tail_refs, out_ref = rest
    subs = _sub_tiles(tm)
    x1s = [x_ref[0, rows] + _dot(o_ref[0, rows], wb_ref[...]) for rows in subs]
    _layer_tail(x1s, tail_refs, out_ref)


def _tail_specs(n_mem):
    D = D_MODEL
    return [
        _const_spec((1, D)),
        _const_spec((D, D)),
        _const_spec((1, MEM_HEAD_DIM)),
        pl.BlockSpec((1, n_mem, D), lambda b, i: (b, 0, 0), pipeline_mode=pl.Buffered(1)),
        pl.BlockSpec((1, n_mem, D), lambda b, i: (b, 0, 0), pipeline_mode=pl.Buffered(1)),
        _const_spec((D, D)),
        _const_spec((1, D)),
        _const_spec((D, D_FF)),
        _const_spec((D_FF, D)),
    ]


def _l0_out(x, up, pool_w, pool_scale, wa, o, wb, tail):
    B, S, D = x.shape
    n_mem = tail[3].shape[1]
    tm = TOKEN_TILE
    halo_blocks = tm // POOL_HALO
    tile = pl.BlockSpec((1, tm, D), lambda b, i: (b, i, 0))
    return pl.pallas_call(
        functools.partial(_l0_out_kernel, tm=tm, seq_len=S),
        grid=(B, S // tm),
        in_specs=[
            tile,
            pl.BlockSpec((1, tm, POOL_WIDTH), lambda b, i: (b, i, 0)),
            pl.BlockSpec((1, POOL_HALO, POOL_WIDTH),
                         lambda b, i: (b, jnp.maximum(i * halo_blocks - 1, 0), 0)),
            pl.BlockSpec((1, POOL_HALO, POOL_WIDTH),
                         lambda b, i: (b, jnp.minimum((i + 1) * halo_blocks, S // POOL_HALO - 1), 0)),
            _const_spec(pool_w.shape),
            _const_spec((1, POOL_WIDTH)),
            _const_spec((POOL_WIDTH, D)),
            pl.BlockSpec((1, tm, o.shape[-1]), lambda b, i: (b, i, 0)),
            _const_spec(wb.shape),
        ] + _tail_specs(n_mem),
        out_specs=tile,
        out_shape=jax.ShapeDtypeStruct((B, S, D), F32),
        scratch_shapes=[pltpu.VMEM((tm + 2 * POOL_HALO, POOL_WIDTH), F32)],
        compiler_params=_params(),
        name="l0_tail",
    )(x, up, up, up, pool_w, pool_scale, wa, o, wb, *tail)


def _l1_out(x, o, wb, tail):
    B, S, D = x.shape
    n_mem = tail[3].shape[1]
    tm = TOKEN_TILE
    tile = pl.BlockSpec((1, tm, D), lambda b, i: (b, i, 0))
    return pl.pallas_call(
        functools.partial(_l1_out_kernel, tm=tm),
        grid=(B, S // tm),
        in_specs=[tile, pl.BlockSpec((1, tm, o.shape[-1]), lambda b, i: (b, i, 0)), _const_spec(wb.shape)]
        + _tail_specs(n_mem),
        out_specs=tile,
        out_shape=jax.ShapeDtypeStruct((B, S, D), F32),
        compiler_params=_params(),
        name="l1_tail",
    )(x, o, wb, *tail)


def _pair_norm(z, g2, lo_mask):
    y = z * z
    lo = jnp.sum(jnp.where(lo_mask, y, 0.0), axis=-1, keepdims=True)
    hi = jnp.sum(jnp.where(lo_mask, 0.0, y), axis=-1, keepdims=True)
    inv = 1.0 / NA_HEAD_DIM
    r = jnp.where(lo_mask, lax.rsqrt(lo * inv + EPS), lax.rsqrt(hi * inv + EPS))
    return z * r * g2


def _l1_front_kernel(x_ref, g_ref, w_ref, qg_ref, kg_ref, q_out, k_out, v_out):
    subs = _sub_tiles(x_ref.shape[1])
    qkvs = [_dot(_rms(x_ref[0, rows], g_ref[...]).astype(BF16), w_ref[...]) for rows in subs]
    width = NA_HEADS * NA_HEAD_DIM
    lo_mask = lax.broadcasted_iota(jnp.int32, (1, LANES), 1) < NA_HEAD_DIM
    scale = NA_HEAD_DIM ** -0.5 * LOG2_E
    for rows, qkv in zip(subs, qkvs):
        for j in range(width // LANES):
            sl = slice(j * LANES, (j + 1) * LANES)
            q_out[0, rows, sl] = (_pair_norm(qkv[:, sl], qg_ref[...], lo_mask) * scale).astype(BF16)
            ksl = slice(width + j * LANES, width + (j + 1) * LANES)
            k_out[0, rows, sl] = _pair_norm(qkv[:, ksl], kg_ref[...], lo_mask).astype(BF16)
        v_out[0, rows] = qkv[:, 2 * width:].astype(BF16)


def _l1_front(x, g, w_qkv, qg2, kg2):
    B, S, D = x.shape
    tm = TOKEN_TILE
    tile = pl.BlockSpec((1, tm, D), lambda b, i: (b, i, 0))
    return pl.pallas_call(
        _l1_front_kernel,
        grid=(B, S // tm),
        in_specs=[tile, _const_spec((1, D)), _const_spec((D, 3 * D)), _const_spec((1, LANES)),
                  _const_spec((1, LANES))],
        out_specs=[tile] * 3,
        out_shape=[jax.ShapeDtypeStruct((B, S, D), BF16)] * 3,
        compiler_params=_params(),
        name="l1_front",
    )(x, g, w_qkv, qg2, kg2)


def _na_bias_tiles(rpb_ref, tab_ref):
    c = lax.broadcasted_iota(jnp.int32, (GRID_W, LANES), 0)
    kc = lax.broadcasted_iota(jnp.int32, (GRID_W, LANES), 1) % GRID_W
    c0 = jnp.clip(c - NA_KW // 2, 0, GRID_W - NA_KW)
    valid = (kc >= c0) & (kc < c0 + NA_KW)
    for h in range(tab_ref.shape[0]):
        for dr in range(tab_ref.shape[1]):
            rows = jnp.broadcast_to(rpb_ref[h, dr:dr + 1, :], (GRID_W, LANES))
            skew = pltpu.roll(rows, LANES - (NA_KW - 1), 1, stride=1, stride_axis=0)
            tab_ref[h, dr] = jnp.where(valid, skew * LOG2_E, MASK_VALUE)


def _na_kernel(q_ref, k_ref, v_ref, rpb_ref, o_ref, tab_ref, *s_refs, rows_per_step, n_rows):
    i = pl.program_id(2)
    win = NA_KH * GRID_W
    lo_mask = lax.broadcasted_iota(jnp.int32, (1, LANES), 1) < NA_HEAD_DIM
    zero = jnp.zeros((), BF16)

    @pl.when(i == 0)
    def _():
        _na_bias_tiles(rpb_ref, tab_ref)

    z_st = pl.multiple_of(jnp.minimum(pl.program_id(2), 0) * SUBLANES, SUBLANES)
    z_ld = pl.multiple_of(jnp.minimum(pl.program_id(1), 0) * SUBLANES, SUBLANES)
    st_rows, ld_rows = pl.ds(z_st, 2 * GRID_W), pl.ds(z_ld, 2 * GRID_W)
    row_max = {}
    ones = jnp.ones((win, LANES), BF16)

    def window(rr):
        r = i * rows_per_step + rr
        r0 = jnp.clip(r - NA_KH // 2, 0, n_rows - NA_KH)
        return r - r0, pl.ds(pl.multiple_of(r0 * GRID_W, GRID_W), win)

    def scores(rr):
        delta, keys = window(rr)
        q_row = q_ref[0, rr * GRID_W:(rr + 1) * GRID_W, :]
        q2 = jnp.concatenate([jnp.where(lo_mask, q_row, zero), jnp.where(lo_mask, zero, q_row)], axis=0)
        bias = jnp.concatenate(
            [jnp.concatenate([tab_ref[h, 2 * p + (NA_KH - 1) - delta] for p in range(NA_KH // 2)], axis=1)
             for h in range(2)], axis=0)
        s = _dot_nt(q2, k_ref[0, keys, :]) + bias
        s_refs[rr][st_rows, :] = s
        row_max[rr] = jnp.max(s, axis=-1, keepdims=True)

    def attend(rr):
        _, keys = window(rr)
        p = jnp.exp2((s_refs[rr][ld_rows, :] - row_max.pop(rr)).astype(BF16))
        pv = _dot(p, jnp.concatenate([v_ref[0, keys, :], ones], axis=1))
        o2 = pv[:, :LANES] * (1.0 / pv[:, LANES:])
        out = jnp.where(lo_mask, o2[:GRID_W], o2[GRID_W:])
        o_ref[0, rr * GRID_W:(rr + 1) * GRID_W, :] = out.astype(BF16)

    for rr in range(min(NA_ROWS_AHEAD, rows_per_step)):
        scores(rr)
    for rr in range(rows_per_step):
        if rr + NA_ROWS_AHEAD < rows_per_step:
            scores(rr + NA_ROWS_AHEAD)
        attend(rr)


def _na_attn(q, k, v, rpb_pairs):
    B, S, D = q.shape
    n_rows = S // GRID_W
    tq = NA_ROWS_PER_STEP * GRID_W
    n_dr = rpb_pairs.shape[1]
    kv_spec = pl.BlockSpec((1, S, LANES), lambda b, j, i: (b, 0, j))
    return pl.pallas_call(
        functools.partial(_na_kernel, rows_per_step=NA_ROWS_PER_STEP, n_rows=n_rows),
        grid=(B, D // LANES, S // tq),
        in_specs=[
            pl.BlockSpec((1, tq, LANES), lambda b, j, i: (b, i, j)),
            kv_spec, kv_spec,
            pl.BlockSpec((2, n_dr, LANES), lambda b, j, i: (j, 0, 0)),
        ],
        out_specs=pl.BlockSpec((1, tq, LANES), lambda b, j, i: (b, i, j)),
        out_shape=jax.ShapeDtypeStruct((B, S, D), BF16),
        scratch_shapes=[pltpu.VMEM((2, n_dr, GRID_W, LANES), F32)]
        + [pltpu.VMEM((2 * GRID_W + SUBLANES, NA_KH * GRID_W), F32)] * NA_ROWS_PER_STEP,
        compiler_params=_params(),
        name="na_attn",
    )(q, k, v, rpb_pairs)


def _na_rpb_pairs(rpb):
    padded = jnp.pad(rpb, ((0, 0), (0, 0), (0, GRID_W - rpb.shape[-1])))
    return jnp.concatenate([padded[:, :-1], padded[:, 1:]], axis=-1)


def _pad_heads(w, heads, width):
    lead = w.shape[:-1]
    w = w.reshape(*lead, heads, width)
    w = jnp.pad(w, [(0, 0)] * len(lead) + [(0, 0), (0, LANES - width)])
    return w.reshape(*lead, heads * LANES)


def _rope_tables(seq_len):
    half = MLA_ROPE // 2
    pos = jnp.arange(seq_len, dtype=F32)
    freqs = ROPE_THETA ** (-jnp.arange(half, dtype=F32) / half)
    ang = pos[:, None] * freqs[None, :]
    cos, sin = jnp.cos(ang), jnp.sin(ang)
    z = lambda n: jnp.zeros((seq_len, n), F32)
    cos_t = jnp.concatenate([jnp.ones((seq_len, MLA_NOPE), F32), cos, cos, z(LANES - MLA_QK)], axis=1)
    sin_a = jnp.concatenate([z(MLA_NOPE), -sin, z(half), z(LANES - MLA_QK)], axis=1)
    sin_b = jnp.concatenate([z(MLA_NOPE), z(half), sin, z(LANES - MLA_QK)], axis=1)
    return cos_t, sin_a, sin_b, cos.T, sin.T


def kernel(x, mem, mix_norm_g, xattn_norm_g, ff_norm_g, w_mem_q, mem_q_g, w_mem_o, w_ff1, w_ff2, mem_tok_norm_g, w_mem_kv, mem_k_g, w_in_e, pool_w, pool_scale, q_lora_g, w_uq, kv_lora_g, w_ukv, mla_q_g, mla_k_g, w_out_e, w_qkv_o, na_q_g, na_k_g, na_rpb, w_out_o):
    B, S, D = x.shape
    row = lambda v: v.reshape(1, -1)

    mem_k, mem_v = _mem_kv(mem, row(mem_tok_norm_g), w_mem_kv.astype(BF16), row(mem_k_g))

    o1 = POOL_WIDTH
    o3 = o1 + Q_LORA + KV_LORA
    w_in = w_in_e[0]
    w_in_pad = jnp.concatenate(
        [w_in[:, :o3], jnp.zeros((D, MLA_NOPE), F32), w_in[:, o3:], jnp.zeros((D, LANES - MLA_QK), F32)],
        axis=1).astype(BF16)
    wuqt = _pad_heads(w_uq[0], MLA_HEADS, MLA_QK).T.astype(BF16)
    w_ukv_h = w_ukv[0].reshape(KV_LORA, MLA_HEADS, MLA_NOPE + MLA_V)
    wuk = _pad_heads(w_ukv_h[:, :, :MLA_NOPE].reshape(KV_LORA, -1), MLA_HEADS, MLA_NOPE).astype(BF16)
    wuvt = _pad_heads(w_ukv_h[:, :, MLA_NOPE:].reshape(KV_LORA, -1), MLA_HEADS, MLA_V).T.astype(BF16)
    qg = jnp.pad(mla_q_g[0], (0, LANES - MLA_QK)) * (MLA_QK ** -0.5 * LOG2_E)
    qg = jnp.broadcast_to(qg[:, None], (LANES, LANES))
    kg = jnp.pad(mla_k_g[0], (0, LANES - MLA_QK)).reshape(1, LANES)
    up, q, k, v = _l0_front(x, row(mix_norm_g[0]), w_in_pad, row(q_lora_g[0]), wuqt, row(kv_lora_g[0]),
                            wuk, wuvt, qg, kg, *_rope_tables(S))
    o = _mla_attn(q, k, v)
    w_out = w_out_e[0]
    wa = w_out[:POOL_WIDTH].astype(BF16)
    wb = jnp.pad(w_out[POOL_WIDTH:].reshape(MLA_HEADS, MLA_V, D), ((0, 0), (0, LANES - MLA_V), (0, 0)))
    wb = wb.reshape(MLA_HEADS * LANES, D).astype(BF16)
    def tail(layer):
        return (row(xattn_norm_g[layer]), w_mem_q[layer].astype(BF16), row(mem_q_g[layer]), mem_k, mem_v,
                w_mem_o[layer].astype(BF16), row(ff_norm_g[layer]), w_ff1[layer].astype(BF16),
                w_ff2[layer].astype(BF16))

    x = _l0_out(x, up, pool_w[0].astype(BF16), row(pool_scale[0]), wa, o, wb, tail(0))

    qg2 = jnp.tile(na_q_g[0], LANES // NA_HEAD_DIM).reshape(1, LANES)
    kg2 = jnp.tile(na_k_g[0], LANES // NA_HEAD_DIM).reshape(1, LANES)
    q, k, v = _l1_front(x, row(mix_norm_g[1]), w_qkv_o[0].astype(BF16), qg2, kg2)
    c = _na_attn(q, k, v, _na_rpb_pairs(na_rpb[0]))
    return _l1_out(x, c, w_out_o[0].astype(BF16), tail(1))
```
